```python
import math
import jax, jax.numpy as jnp
from jax import lax
import numpy as np

D_MODEL = 1024
BATCH = 32
SEQ = 2048
DEPTH = 1
DEC_BATCH = 128
DEC_SEQ = 1
PAST_LEN = 16384
PAGE_SIZE = 128

MLA_HEADS = 8
MLA_NOPE = 64
MLA_ROPE = 32
MLA_QK = MLA_NOPE + MLA_ROPE
MLA_VD = 64
KV_RANK = 256
ROPE_THETA = 10000.0
DIFF_HEADS = 4
DIFF_HD = 64
DIFF_VD = 2 * DIFF_HD
MEM_HEADS = 4
MEM_HD = 128
N_MEM = 256
N_BUCKETS = 32
MAX_EXACT = N_BUCKETS // 2
MAX_DISTANCE = 128
N_EXPERTS = 32
TOP_K = 4
D_EXPERT = D_MODEL
SWIGLU_LIMIT = 7.0
SWIGLU_ALPHA = 1.702
MOE_BLOCK = 128
EPS = 1e-6
Q_BLOCK = 128
N_BRANCHES = 3
SPLIT_SIZES = (MLA_HEADS * MLA_QK, KV_RANK, MLA_ROPE, DIFF_HEADS * 2 * DIFF_HD, DIFF_HEADS * 2 * DIFF_HD,
               DIFF_HEADS * DIFF_VD, MEM_HEADS * MEM_HD, N_BRANCHES * D_MODEL)
IN_WIDTH = sum(SPLIT_SIZES)

kernel_name = 'hybrid_mla_diffattn_memxattn_moe_step'


def _rmsnorm(x, g):
    xf = x.astype(jnp.float32)
    y = xf * lax.rsqrt(jnp.mean(xf * xf, axis=-1, keepdims=True) + EPS)
    return (y * g.astype(jnp.float32)).astype(x.dtype)


def _rope_tables(pos):
    inv = ROPE_THETA ** (-jnp.arange(0, MLA_ROPE, 2, dtype=jnp.float32) / MLA_ROPE)
    ang = pos.astype(jnp.float32)[:, None] * inv[None, :]
    return jnp.cos(ang), jnp.sin(ang)


def _apply_rope(x, cos, sin):
    xf = x.astype(jnp.float32)
    x1, x2 = jnp.split(xf, 2, axis=-1)
    return jnp.concatenate([x1 * cos - x2 * sin, x1 * sin + x2 * cos], axis=-1).astype(x.dtype)


def _t5_bias(table, q_pos, k_pos):
    n = jnp.maximum(q_pos[:, None] - k_pos[None, :], 0)
    nf = jnp.maximum(n, 1).astype(jnp.float32)
    large = MAX_EXACT + (jnp.log(nf / MAX_EXACT) / math.log(MAX_DISTANCE / MAX_EXACT)
                         * (N_BUCKETS - MAX_EXACT)).astype(jnp.int32)
    bucket = jnp.where(n < MAX_EXACT, n, jnp.minimum(large, N_BUCKETS - 1))
    return jnp.moveaxis(table.astype(jnp.float32)[bucket], -1, 0)


def _online_update(m, l, acc, s, pv):
    m_new = jnp.maximum(m, jnp.max(s, axis=-1))
    alpha = jnp.exp(m - m_new)
    p = jnp.exp(s - m_new[..., None])
    return m_new, alpha * l + jnp.sum(p, axis=-1), alpha[..., None] * acc + pv(p)


def _project(x, pos, lp):
    n, t = x.shape[0], x.shape[1]
    h = _rmsnorm(x, lp['g_attn'])
    z = jnp.einsum('ntd,de->nte', h, lp['w_in'])
    cuts = [int(c) for c in np.cumsum(SPLIT_SIZES)[:-1]]
    q_mla, c_kv, k_r, q_d, k_d, v_d, q_m, g = jnp.split(z, cuts, axis=-1)
    cos, sin = _rope_tables(pos)
    q_mla = _rmsnorm(q_mla.reshape(n, t, MLA_HEADS, MLA_QK), lp['g_q_mla'])
    q_nope = q_mla[..., :MLA_NOPE]
    q_rope = _apply_rope(q_mla[..., MLA_NOPE:], cos[:, None, :], sin[:, None, :])
    c_kv = _rmsnorm(c_kv, lp['g_ckv'])
    k_rope = _apply_rope(_rmsnorm(k_r, lp['g_krope']), cos, sin)
    q_d = _rmsnorm(q_d.reshape(n, t, DIFF_HEADS, 2, DIFF_HD), lp['g_q_diff'])
    k_d = _rmsnorm(k_d.reshape(n, t, DIFF_HEADS, 2, DIFF_HD), lp['g_k_diff'])
    v_d = v_d.reshape(n, t, DIFF_HEADS, DIFF_VD)
    q_m = _rmsnorm(q_m.reshape(n, t, MEM_HEADS, MEM_HD), lp['g_q_mem'])
    gates = jax.nn.sigmoid(g.reshape(n, t, N_BRANCHES, D_MODEL))
    return q_nope, q_rope, c_kv, k_rope, q_d, k_d, v_d, q_m, gates


def _mla_prompt(q_nope, q_rope, c_kv, k_rope, w_uk, w_uv):
    n, s_len = q_nope.shape[0], q_nope.shape[1]
    scale = MLA_QK ** -0.5
    k_nope = jnp.einsum('nsc,chd->nshd', c_kv, w_uk)
    v = jnp.einsum('nsc,chd->nshd', c_kv, w_uv)
    k_pos = jnp.arange(s_len)

    def block(i):
        q0 = i * Q_BLOCK
        qn = lax.dynamic_slice_in_dim(q_nope, q0, Q_BLOCK, axis=1)
        qr = lax.dynamic_slice_in_dim(q_rope, q0, Q_BLOCK, axis=1)
        s = (jnp.einsum('nqhd,nkhd->nhqk', qn, k_nope)
             + jnp.einsum('nqhr,nkr->nhqk', qr, k_rope)).astype(jnp.float32) * scale
        mask = (q0 + jnp.arange(Q_BLOCK))[:, None] >= k_pos[None, :]
        p = jax.nn.softmax(jnp.where(mask, s, -jnp.inf), axis=-1).astype(v.dtype)
        return jnp.einsum('nhqk,nkhd->nqhd', p, v)

    o = lax.map(block, jnp.arange(s_len // Q_BLOCK))
    return jnp.moveaxis(o, 0, 1).reshape(n, s_len, MLA_HEADS, MLA_VD)


def _mla_sample(q_nope, q_rope, c_kv, k_rope, cache_lat, cache_kr, layer, page_table, w_uk, w_uv):
    n, t = q_nope.shape[0], q_nope.shape[1]
    scale = MLA_QK ** -0.5
    q_lat = jnp.einsum('nthd,chd->nthc', q_nope, w_uk)

    def scores(lat, kr):
        s = jnp.einsum('nthc,npc->nhtp', q_lat, lat) + jnp.einsum('nthr,npr->nhtp', q_rope, kr)
        return s.astype(jnp.float32) * scale

    def pv(lat):
        return lambda p: jnp.einsum('nhtp,npc->nhtc', p, lat.astype(jnp.float32))

    def step(carry, pages):
        lat = cache_lat[layer, pages]
        kr = cache_kr[layer, pages]
        return _online_update(*carry, scores(lat, kr), pv(lat)), None

    init = (jnp.full((n, MLA_HEADS, t), -jnp.inf, jnp.float32),
            jnp.zeros((n, MLA_HEADS, t), jnp.float32),
            jnp.zeros((n, MLA_HEADS, t, KV_RANK), jnp.float32))
    carry, _ = lax.scan(step, init, page_table.T)
    pos = jnp.arange(t)
    s = jnp.where(pos[:, None] >= pos[None, :], scores(c_kv, k_rope), -jnp.inf)
    m, l, acc = _online_update(*carry, s, pv(c_kv))
    o_lat = (acc / l[..., None]).astype(q_nope.dtype)
    return jnp.einsum('nhtc,chd->nthd', o_lat, w_uv)


def _diff_prompt(q, k, v, t5_table):
    n, s_len = q.shape[0], q.shape[1]
    scale = DIFF_HD ** -0.5
    k_pos = jnp.arange(s_len)

    def block(i):
        q0 = i * Q_BLOCK
        qb = lax.dynamic_slice_in_dim(q, q0, Q_BLOCK, axis=1)
        q_pos = q0 + jnp.arange(Q_BLOCK)
        s = jnp.einsum('nqhmd,nkhmd->nhmqk', qb, k).astype(jnp.float32) * scale
        s = s + _t5_bias(t5_table, q_pos, k_pos)[:, None]
        mask = q_pos[:, None] >= k_pos[None, :]
        p = jax.nn.softmax(jnp.where(mask, s, -jnp.inf), axis=-1).astype(v.dtype)
        return jnp.einsum('nhmqk,nkhd->nqhmd', p, v)

    o = lax.map(block, jnp.arange(s_len // Q_BLOCK))
    return jnp.moveaxis(o, 0, 1).reshape(n, s_len, DIFF_HEADS, 2, DIFF_VD)


def _diff_sample(q, k_new, v_new, cache_k, cache_v, layer, page_table, t5_table):
    n, t = q.shape[0], q.shape[1]
    n_pages = page_table.shape[1]
    q_pos = n_pages * PAGE_SIZE + jnp.arange(t)
    scale = DIFF_HD ** -0.5

    def scores(k, k_pos):
        s = jnp.einsum('nthmd,nphmd->nhmtp', q, k).astype(jnp.float32) * scale
        return s + _t5_bias(t5_table, q_pos, k_pos)[:, None]

    def pv(v):
        return lambda p: jnp.einsum('nhmtp,nphd->nhmtd', p, v.astype(jnp.float32))

    def step(carry, xs):
        j, pages = xs
        k = cache_k[layer, pages]
        v = cache_v[layer, pages]
        s = scores(k, j * PAGE_SIZE + jnp.arange(PAGE_SIZE))
        return _online_update(*carry, s, pv(v)), None

    init = (jnp.full((n, DIFF_HEADS, 2, t), -jnp.inf, jnp.float32),
            jnp.zeros((n, DIFF_HEADS, 2, t), jnp.float32),
            jnp.zeros((n, DIFF_HEADS, 2, t, DIFF_VD), jnp.float32))
    carry, _ = lax.scan(step, init, (jnp.arange(n_pages), page_table.T))
    s = jnp.where(q_pos[:, None] >= q_pos[None, :], scores(k_new, q_pos), -jnp.inf)
    m, l, acc = _online_update(*carry, s, pv(v_new))
    o = acc / l[..., None]
    return jnp.transpose(o, (0, 3, 1, 2, 4)).astype(q.dtype)


def _diff_combine(o, lam, lam_init, g_subln):
    o = o[..., 0, :] - lam.astype(o.dtype) * o[..., 1, :]
    return _rmsnorm(o, g_subln) * (1.0 - lam_init)


def _memory_kv(mem, lp):
    n, m_len = mem.shape[0], mem.shape[1]
    hm = _rmsnorm(mem, lp['g_mem'])
    kv = jnp.einsum('nmd,de->nme', hm, lp['w_mem_kv']).reshape(n, m_len, 2, MEM_HEADS, MEM_HD)
    return _rmsnorm(kv[:, :, 0], lp['g_k_mem']), kv[:, :, 1]


def _memory_attend(q_m, mem_k, mem_v):
    s = jnp.einsum('nthd,nmhd->nhtm', q_m, mem_k).astype(jnp.float32) * (MEM_HD ** -0.5)
    p = jax.nn.softmax(s, axis=-1).astype(mem_v.dtype)
    return jnp.einsum('nhtm,nmhd->nthd', p, mem_v)


def _moe(x, lp):
    n, t, d = x.shape
    n_tok = n * t
    h = _rmsnorm(x, lp['g_ffn']).reshape(n_tok, d)
    logits = (h @ lp['w_router'] + lp['b_router']).astype(jnp.float32)
    top_val, top_idx = lax.top_k(logits, TOP_K)
    gate_w = jax.nn.softmax(top_val, axis=-1)
    n_slots = n_tok * TOP_K
    e_flat = top_idx.reshape(-1)
    order = jnp.argsort(e_flat)
    e_sorted = e_flat[order]
    tok_sorted = (order // TOP_K).astype(jnp.int32)
    w_sorted = gate_w.reshape(-1)[order]
    counts = jnp.bincount(e_flat, length=N_EXPERTS)
    padded = (counts + MOE_BLOCK - 1) // MOE_BLOCK * MOE_BLOCK
    start = jnp.cumsum(counts) - counts
    pend = jnp.cumsum(padded)
    pstart = pend - padded
    dest = pstart[e_sorted] + jnp.arange(n_slots) - start[e_sorted]
    n_blocks = (n_slots + N_EXPERTS * (MOE_BLOCK - 1) + MOE_BLOCK - 1) // MOE_BLOCK
    n_rows = n_blocks * MOE_BLOCK
    row_tok = jnp.full((n_rows,), n_tok, jnp.int32).at[dest].set(tok_sorted)
    row_w = jnp.zeros((n_rows,), jnp.float32).at[dest].set(w_sorted)
    block_expert = jnp.minimum(jnp.searchsorted(pend, jnp.arange(n_blocks) * MOE_BLOCK, side='right'),
                               N_EXPERTS - 1)
    h_pad = jnp.concatenate([h, jnp.zeros((1, d), h.dtype)], axis=0)

    def expert_block(args):
        rows, e = args
        xb = h_pad[rows]
        gu = xb @ lp['w_gate_up'][e] + lp['b_gate_up'][e]
        gate = jnp.minimum(gu[:, :D_EXPERT], SWIGLU_LIMIT)
        up = jnp.clip(gu[:, D_EXPERT:], -SWIGLU_LIMIT, SWIGLU_LIMIT)
        act = (up + 1.0) * (gate * jax.nn.sigmoid(SWIGLU_ALPHA * gate))
        return act @ lp['w_down'][e] + lp['b_down'][e]

    out = lax.map(expert_block, (row_tok.reshape(n_blocks, MOE_BLOCK), block_expert))
    out = out.reshape(n_rows, d) * row_w[:, None].astype(out.dtype)
    y = jnp.zeros((n_tok + 1, d), out.dtype).at[row_tok].add(out)[:n_tok]
    return y.reshape(n, t, d).astype(x.dtype)


def _merge(x, gates, o_mla, o_diff, o_mem, lp):
    n, t = x.shape[0], x.shape[1]
    b_mla = jnp.einsum('nte,ed->ntd', o_mla.reshape(n, t, -1), lp['w_br_mla'])
    b_diff = jnp.einsum('nte,ed->ntd', o_diff.reshape(n, t, -1), lp['w_br_diff'])
    b_mem = jnp.einsum('nte,ed->ntd', o_mem.reshape(n, t, -1), lp['w_br_mem'])
    mixed = gates[:, :, 0] * b_mla + gates[:, :, 1] * b_diff + gates[:, :, 2] * b_mem
    x = x + jnp.einsum('ntd,de->nte', mixed, lp['w_out'])
    return x + _moe(x, lp)


def setup_inputs(seed: int = 0) -> dict:
    key = jax.random.key(seed)
    ks = iter(jax.random.split(key, 64))
    f32 = jnp.float32

    def nrm(shape, scale):
        return jax.random.normal(next(ks), shape, f32) * scale

    def gain(shape):
        return 1.0 + nrm(shape, 0.02)

    n_pages = PAST_LEN // PAGE_SIZE
    n_used = DEC_BATCH * n_pages
    n_pool = n_used + n_used // 4
    page_table = jax.random.permutation(next(ks), n_pool)[:n_used].reshape(DEC_BATCH, n_pages).astype(jnp.int32)
    return {
        'x_prompt': nrm((BATCH, SEQ, D_MODEL), 1.0),
        'x_sample': nrm((DEC_BATCH, DEC_SEQ, D_MODEL), 1.0),
        'mem_prompt': nrm((BATCH, N_MEM, D_MODEL), 1.0),
        'cache_mla_latent': nrm((DEPTH, n_pool, PAGE_SIZE, KV_RANK), 1.0),
        'cache_mla_krope': nrm((DEPTH, n_pool, PAGE_SIZE, MLA_ROPE), 1.0),
        'cache_diff_k': nrm((DEPTH, n_pool, PAGE_SIZE, DIFF_HEADS, 2, DIFF_HD), 1.0),
        'cache_diff_v': nrm((DEPTH, n_pool, PAGE_SIZE, DIFF_HEADS, DIFF_VD), 1.0),
        'cache_mem_k': nrm((DEPTH, DEC_BATCH, N_MEM, MEM_HEADS, MEM_HD), 1.0),
        'cache_mem_v': nrm((DEPTH, DEC_BATCH, N_MEM, MEM_HEADS, MEM_HD), 1.0),
        'page_table': page_table,
        't5_bias': nrm((N_BUCKETS, DIFF_HEADS), 0.3),
        'g_attn': gain((DEPTH, D_MODEL)),
        'w_in': nrm((DEPTH, D_MODEL, IN_WIDTH), D_MODEL ** -0.5),
        'g_q_mla': gain((DEPTH, MLA_QK)),
        'g_ckv': gain((DEPTH, KV_RANK)),
        'g_krope': gain((DEPTH, MLA_ROPE)),
        'w_uk': nrm((DEPTH, KV_RANK, MLA_HEADS, MLA_NOPE), KV_RANK ** -0.5),
        'w_uv': nrm((DEPTH, KV_RANK, MLA_HEADS, MLA_VD), KV_RANK ** -0.5),
        'g_q_diff': gain((DEPTH, DIFF_HD)),
        'g_k_diff': gain((DEPTH, DIFF_HD)),
        'lambda_q1': nrm((DEPTH, DIFF_HD), 0.1),
        'lambda_k1': nrm((DEPTH, DIFF_HD), 0.1),
        'lambda_q2': nrm((DEPTH, DIFF_HD), 0.1),
        'lambda_k2': nrm((DEPTH, DIFF_HD), 0.1),
        'g_subln': gain((DEPTH, DIFF_VD)),
        'g_mem': gain((DEPTH, D_MODEL)),
        'w_mem_kv': nrm((DEPTH, D_MODEL, 2 * MEM_HEADS * MEM_HD), D_MODEL ** -0.5),
        'g_q_mem': gain((DEPTH, MEM_HD)),
        'g_k_mem': gain((DEPTH, MEM_HD)),
        'w_br_mla': nrm((DEPTH, MLA_HEADS * MLA_VD, D_MODEL), (MLA_HEADS * MLA_VD) ** -0.5),
        'w_br_diff': nrm((DEPTH, DIFF_HEADS * DIFF_VD, D_MODEL), (DIFF_HEADS * DIFF_VD) ** -0.5),
        'w_br_mem': nrm((DEPTH, MEM_HEADS * MEM_HD, D_MODEL), (MEM_HEADS * MEM_HD) ** -0.5),
        'w_out': nrm((DEPTH, D_MODEL, D_MODEL), D_MODEL ** -0.5),
        'g_ffn': gain((DEPTH, D_MODEL)),
        'w_router': nrm((DEPTH, D_MODEL, N_EXPERTS), D_MODEL ** -0.5),
        'b_router': nrm((DEPTH, N_EXPERTS), 0.01),
        'w_gate_up': nrm((DEPTH, N_EXPERTS, D_MODEL, 2 * D_EXPERT), D_MODEL ** -0.5),
        'b_gate_up': nrm((DEPTH, N_EXPERTS, 2 * D_EXPERT), 0.01),
        'w_down': nrm((DEPTH, N_EXPERTS, D_EXPERT, D_MODEL), D_EXPERT ** -0.5),
        'b_down': nrm((DEPTH, N_EXPERTS, D_MODEL), 0.01),
    }


def reference(x_prompt, x_sample, mem_prompt, cache_mla_latent, cache_mla_krope, cache_diff_k, cache_diff_v,
              cache_mem_k, cache_mem_v, page_table, t5_bias, g_attn, w_in, g_q_mla, g_ckv, g_krope, w_uk, w_uv,
              g_q_diff, g_k_diff, lambda_q1, lambda_k1, lambda_q2, lambda_k2, g_subln, g_mem, w_mem_kv, g_q_mem,
              g_k_mem, w_br_mla, w_br_diff, w_br_mem, w_out, g_ffn, w_router, b_router, w_gate_up, b_gate_up,
              w_down, b_down):
    pos_p = jnp.arange(x_prompt.shape[1])
    pos_s = page_table.shape[1] * PAGE_SIZE + jnp.arange(x_sample.shape[1])
    hp, hs = x_prompt, x_sample
    lat_p, kr_p, dk_p, dv_p, mk_p, mv_p = [], [], [], [], [], []
    lat_s, kr_s, dk_s, dv_s = [], [], [], []
    for l in range(DEPTH):
        lp = dict(g_attn=g_attn[l], w_in=w_in[l], g_q_mla=g_q_mla[l], g_ckv=g_ckv[l], g_krope=g_krope[l],
                  g_q_diff=g_q_diff[l], g_k_diff=g_k_diff[l], g_mem=g_mem[l], w_mem_kv=w_mem_kv[l],
                  g_q_mem=g_q_mem[l], g_k_mem=g_k_mem[l], w_br_mla=w_br_mla[l], w_br_diff=w_br_diff[l],
                  w_br_mem=w_br_mem[l], w_out=w_out[l], g_ffn=g_ffn[l], w_router=w_router[l],
                  b_router=b_router[l], w_gate_up=w_gate_up[l], b_gate_up=b_gate_up[l], w_down=w_down[l],
                  b_down=b_down[l])
        lam_init = 0.8 - 0.6 * math.exp(-0.3 * l)
        lam = (jnp.exp(jnp.sum(lambda_q1[l] * lambda_k1[l]).astype(jnp.float32))
               - jnp.exp(jnp.sum(lambda_q2[l] * lambda_k2[l]).astype(jnp.float32)) + lam_init)

        q_nope, q_rope, c_kv, k_rope, q_d, k_d, v_d, q_m, gates = _project(hp, pos_p, lp)
        o_mla = _mla_prompt(q_nope, q_rope, c_kv, k_rope, w_uk[l], w_uv[l])
        o_diff = _diff_combine(_diff_prompt(q_d, k_d, v_d, t5_bias), lam, lam_init, g_subln[l])
        mem_k, mem_v = _memory_kv(mem_prompt, lp)
        o_mem = _memory_attend(q_m, mem_k, mem_v)
        hp = _merge(hp, gates, o_mla, o_diff, o_mem, lp)
        lat_p.append(c_kv); kr_p.append(k_rope); dk_p.append(k_d); dv_p.append(v_d)
        mk_p.append(mem_k); mv_p.append(mem_v)

        sq_nope, sq_rope, s_ckv, s_krope, sq_d, sk_d, sv_d, sq_m, s_gates = _project(hs, pos_s, lp)
        so_mla = _mla_sample(sq_nope, sq_rope, s_ckv, s_krope, cache_mla_latent, cache_mla_krope, l,
                             page_table, w_uk[l], w_uv[l])
        so_diff = _diff_combine(_diff_sample(sq_d, sk_d, sv_d, cache_diff_k, cache_diff_v, l, page_table, t5_bias),
                                lam, lam_init, g_subln[l])
        so_mem = _memory_attend(sq_m, cache_mem_k[l], cache_mem_v[l])
        hs = _merge(hs, s_gates, so_mla, so_diff, so_mem, lp)
        lat_s.append(s_ckv); kr_s.append(s_krope); dk_s.append(sk_d); dv_s.append(sv_d)

    new_mla_latent_prompt = jnp.stack(lat_p)
    new_mla_krope_prompt = jnp.stack(kr_p)
    new_diff_k_prompt = jnp.stack(dk_p)
    new_diff_v_prompt = jnp.stack(dv_p)
    new_mem_k_prompt = jnp.stack(mk_p)
    new_mem_v_prompt = jnp.stack(mv_p)
    new_mla_latent_sample = jnp.stack(lat_s)
    new_mla_krope_sample = jnp.stack(kr_s)
    new_diff_k_sample = jnp.stack(dk_s)
    new_diff_v_sample = jnp.stack(dv_s)
    return (hp, hs, new_mla_latent_prompt, new_mla_krope_prompt, new_diff_k_prompt, new_diff_v_prompt,
            new_mem_k_prompt, new_mem_v_prompt, new_mla_latent_sample, new_mla_krope_sample,
            new_diff_k_sample, new_diff_v_sample)
```

```python
import functools
import math

import jax
import jax.numpy as jnp
import numpy as np
from jax import lax
from jax.experimental import pallas as pl
from jax.experimental.pallas import tpu as pltpu

D_MODEL = 1024
PAGE_SIZE = 128
MLA_HEADS = 8
MLA_NOPE = 64
MLA_ROPE = 32
MLA_QK = MLA_NOPE + MLA_ROPE
MLA_VD = 64
KV_RANK = 256
ROPE_THETA = 10000.0
DIFF_HEADS = 4
DIFF_HD = 64
DIFF_VD = 2 * DIFF_HD
MEM_HEADS = 4
MEM_HD = 128
N_BUCKETS = 32
MAX_EXACT = N_BUCKETS // 2
MAX_DISTANCE = 128
N_EXPERTS = 32
TOP_K = 4
D_EXPERT = D_MODEL
SWIGLU_LIMIT = 7.0
SWIGLU_ALPHA = 1.702
EPS = 1e-6
N_BRANCHES = 3
LAM_INIT = 0.8 - 0.6 * math.exp(-0.3 * 0)

LANES = 128
VMEM_LIMIT = 56 * 1024 * 1024
NEG_BIG = -1e30
TILE_PROJ = 256
TILE_ATTN = 512
TILE_MERGE = 256
TILE_TOKEN = 256
TILE_EXPERT = 512
TILE_EXPERT_SAMPLE = 128
PAGES_PER_STEP = 8

C_QM = 0
C_CKV = C_QM + MLA_HEADS * LANES
C_KR = C_CKV + KV_RANK
C_QD = C_KR + LANES
C_KD = C_QD + 512
C_VD = C_KD + 512
C_QMEM = C_VD + 512
C_END = C_QMEM + 512

BF = jnp.bfloat16
F32 = jnp.float32


def _cparams(sem):
    return pltpu.CompilerParams(dimension_semantics=sem, vmem_limit_bytes=VMEM_LIMIT)


def _const_spec(shape):
    nd = len(shape)
    return pl.BlockSpec(shape, lambda *a: (0,) * nd)


def _rsqrt_mean(sumsq, n):
    return lax.rsqrt(sumsq * (1.0 / n) + EPS)


def _proj_kernel(x_ref, w_ref, gattn_ref, gq_ref, gckv_ref, gkr_ref, gqd_ref, gkd_ref, gqm_ref,
                 cq_ref, sq_ref, ck_ref, sk_ref, *rest, prompt):
    if prompt:
        (wuk_ref, wuv_ref, qm_o, ckv_o, kr_o, qd_o, kd_o, kdb_o, vd_o, vdb_o, qmem_o, kmla_o, vmla_o) = rest
    else:
        (qm_o, ckv_o, kr_o, qd_o, kd_o, kdb_o, vd_o, vdb_o, qmem_o) = rest
    x = x_ref[...]
    r = _rsqrt_mean(jnp.sum(x * x, axis=-1, keepdims=True), D_MODEL)
    h = (x * r * gattn_ref[...]).astype(BF)
    lane = lax.broadcasted_iota(jnp.int32, (1, LANES), 1)
    lo64 = lane < 64

    def seg(c0, n):
        return jnp.dot(h, w_ref[:, c0:c0 + n], preferred_element_type=F32)

    zq = seg(C_QM, MLA_HEADS * LANES)
    cq, sq = cq_ref[...], sq_ref[...]
    gq = gq_ref[...]
    for g in range(MLA_HEADS):
        z = zq[:, g * LANES:(g + 1) * LANES]
        zm = jnp.where(lane < MLA_QK, z, 0.0)
        rr = _rsqrt_mean(jnp.sum(zm * zm, axis=-1, keepdims=True), MLA_QK)
        y = zm * rr * gq
        yrot = pltpu.roll(z, LANES - MLA_ROPE, 1) * rr
        qm_o[:, g * LANES:(g + 1) * LANES] = (y * cq + yrot * sq).astype(BF)

    zc = seg(C_CKV, KV_RANK)
    rr = _rsqrt_mean(jnp.sum(zc * zc, axis=-1, keepdims=True), KV_RANK)
    ckv = zc * rr * gckv_ref[...]
    ckv_o[...] = ckv

    zk = seg(C_KR, LANES)
    zm = jnp.where(lane < MLA_ROPE, zk, 0.0)
    rr = _rsqrt_mean(jnp.sum(zm * zm, axis=-1, keepdims=True), MLA_ROPE)
    kr = zm * rr * gkr_ref[...] * ck_ref[...] + pltpu.roll(zk, LANES - MLA_ROPE, 1) * rr * sk_ref[...]
    kr_o[...] = kr[:, :MLA_ROPE]

    def halfnorm(z, gain):
        sq_ = z * z
        s_lo = jnp.sum(jnp.where(lo64, sq_, 0.0), axis=-1, keepdims=True)
        s_hi = jnp.sum(jnp.where(lo64, 0.0, sq_), axis=-1, keepdims=True)
        rr_ = jnp.where(lo64, _rsqrt_mean(s_lo, DIFF_HD), _rsqrt_mean(s_hi, DIFF_HD))
        return z * rr_ * gain

    zqd = seg(C_QD, 512)
    gqd = gqd_ref[...]
    for hh in range(DIFF_HEADS):
        y = halfnorm(zqd[:, hh * LANES:(hh + 1) * LANES], gqd)
        qd_o[:, (2 * hh) * LANES:(2 * hh + 1) * LANES] = jnp.where(lo64, y, 0.0).astype(BF)
        qd_o[:, (2 * hh + 1) * LANES:(2 * hh + 2) * LANES] = jnp.where(lo64, 0.0, y).astype(BF)

    zkd = seg(C_KD, 512)
    gkd = gkd_ref[...]
    for hh in range(DIFF_HEADS):
        y = halfnorm(zkd[:, hh * LANES:(hh + 1) * LANES], gkd)
        kd_o[:, hh * LANES:(hh + 1) * LANES] = y
        kdb_o[:, hh * LANES:(hh + 1) * LANES] = y.astype(BF)

    zv = seg(C_VD, 512)
    vd_o[...] = zv
    vdb_o[...] = zv.astype(BF)

    zm_ = seg(C_QMEM, 512)
    gqm = gqm_ref[...]
    for hh in range(MEM_HEADS):
        z = zm_[:, hh * LANES:(hh + 1) * LANES]
        rr = _rsqrt_mean(jnp.sum(z * z, axis=-1, keepdims=True), MEM_HD)
        qmem_o[:, hh * LANES:(hh + 1) * LANES] = (z * rr * gqm).astype(BF)

    if prompt:
        cb = ckv.astype(BF)
        kn = jnp.dot(cb, wuk_ref[...], preferred_element_type=F32)
        krs = pltpu.roll(kr, MLA_NOPE, 1)
        for g in range(MLA_HEADS):
            kmla_o[:, g * LANES:(g + 1) * LANES] = (kn[:, g * LANES:(g + 1) * LANES] + krs).astype(BF)
        vmla_o[...] = jnp.dot(cb, wuv_ref[...], preferred_element_type=F32).astype(BF)


def _rope_lane_tables(pos):
    inv = ROPE_THETA ** (-jnp.arange(0, MLA_ROPE, 2, dtype=F32) / MLA_ROPE)
    ang = pos.astype(F32)[:, None] * inv[None, :]
    cos, sin = jnp.cos(ang), jnp.sin(ang)
    n = pos.shape[0]
    one = jnp.ones((n, MLA_NOPE), F32)
    z32 = jnp.zeros((n, 32), F32)
    z64 = jnp.zeros((n, 64), F32)
    cq = jnp.concatenate([one, cos, cos, z32], axis=1)
    sq = jnp.concatenate([z64, sin, sin, z32], axis=1)
    ck = jnp.concatenate([cos, cos, z32, z64], axis=1)
    sk = jnp.concatenate([sin, sin, z32, z64], axis=1)
    return cq, sq, ck, sk


def _rot_cols(w, g):
    half = MLA_ROPE // 2
    return jnp.concatenate([-w[..., half:] * g[half:], w[..., :half] * g[:half]], axis=-1)


def _pack_proj_weights(w_in, g_q_mla, g_krope):
    d = w_in.shape[0]
    cuts = np.cumsum([MLA_HEADS * MLA_QK, KV_RANK, MLA_ROPE, 512, 512, 512, 512])
    wq = w_in[:, :cuts[0]].reshape(d, MLA_HEADS, MLA_QK)
    scale = MLA_QK ** -0.5
    rotq = _rot_cols(wq[..., MLA_NOPE:], g_q_mla[MLA_NOPE:] * scale)
    wq = jnp.concatenate([wq, rotq], axis=-1).reshape(d, MLA_HEADS * LANES)
    wc = w_in[:, cuts[0]:cuts[1]]
    wk = w_in[:, cuts[1]:cuts[2]]
    wk = jnp.concatenate([wk, _rot_cols(wk, g_krope), jnp.zeros((d, 64), F32)], axis=-1)
    rest = w_in[:, cuts[2]:cuts[6]]
    wp = jnp.concatenate([wq, wc, wk, rest], axis=1).astype(BF)
    wg = w_in[:, cuts[6]:].astype(BF)
    return wp, wg


def _pad_lanes(v, n=LANES):
    return jnp.pad(v, (0, n - v.shape[0])).reshape(1, n)


def _project(x2d, pos_tab, wp, gains, tm, prompt, wuk=None, wuv=None):
    t = x2d.shape[0]
    cq, sq, ck, sk = pos_tab
    nper = cq.shape[0] // tm
    row = lambda w: pl.BlockSpec((tm, w), lambda i: (i, 0))
    tab = pl.BlockSpec((tm, LANES), lambda i: (i % nper, 0))
    ins = [x2d, wp] + list(gains) + [cq, sq, ck, sk]
    in_specs = ([row(D_MODEL), _const_spec(wp.shape)] + [_const_spec(g.shape) for g in gains] + [tab] * 4)
    outs = [(1024, BF), (KV_RANK, F32), (MLA_ROPE, F32), (1024, BF), (512, F32), (512, BF), (512, F32), (512, BF),
            (512, BF)]
    if prompt:
        ins += [wuk, wuv]
        in_specs += [_const_spec(wuk.shape), _const_spec(wuv.shape)]
        outs += [(1024, BF), (1024, BF)]
    return pl.pallas_call(
        functools.partial(_proj_kernel, prompt=prompt),
        grid=(t // tm,),
        in_specs=in_specs,
        out_specs=[row(w) for w, _ in outs],
        out_shape=[jax.ShapeDtypeStruct((t, w), dt) for w, dt in outs],
        compiler_params=_cparams(("parallel",)),
        name="proj_prompt" if prompt else "proj_sample",
    )(*ins)


def _flash_kernel(q_ref, k_ref, v_ref, *rest, causal, has_bias, tq, tk, nk):
    if has_bias:
        bias_ref, o_ref, m_sc, l_sc, acc_sc = rest
    else:
        o_ref, m_sc, l_sc, acc_sc = rest
    i = pl.program_id(2)
    q = q_ref[...]
    m_sc[...] = jnp.full(m_sc.shape, -jnp.inf, F32)
    l_sc[...] = jnp.zeros(l_sc.shape, F32)
    acc_sc[...] = jnp.zeros(acc_sc.shape, F32)

    def step(j, masked):
        off = pl.multiple_of(j * tk, tk)
        k = k_ref[pl.ds(off, tk), :]
        v = v_ref[pl.ds(off, tk), :]
        s = lax.dot_general(q, k, (((1,), (1,)), ((), ())), preferred_element_type=F32)
        if has_bias:
            s = s + bias_ref[jnp.minimum(i - j, 2)]
        elif masked:
            rows = lax.broadcasted_iota(jnp.int32, (tq, tk), 0)
            cols = lax.broadcasted_iota(jnp.int32, (tq, tk), 1)
            s = jnp.where(rows >= cols, s, NEG_BIG)
        m_prev = m_sc[...]
        m_new = jnp.maximum(m_prev, jnp.max(s, axis=-1, keepdims=True))
        alpha = jnp.exp(m_prev - m_new)
        p = jnp.exp(s - m_new)
        l_sc[...] = alpha * l_sc[...] + jnp.sum(p, axis=-1, keepdims=True)
        acc_sc[...] = alpha * acc_sc[...] + jnp.dot(p.astype(BF), v, preferred_element_type=F32)
        m_sc[...] = m_new

    if causal:
        def body(j, c):
            step(j, False)
            return c
        lax.fori_loop(0, i, body, 0)
        step(i, True)
    else:
        for j in range(nk):
            step(j, False)
    o_ref[...] = (acc_sc[...] / l_sc[...]).astype(o_ref.dtype)


def _flash(q, k, v, bias, *, causal, group, tq, tk, out_dtype, name):
    n, sq_len, hq = q.shape[0], q.shape[1], q.shape[2] // LANES
    sk_len = k.shape[1]
    has_bias = bias is not None
    in_specs = [pl.BlockSpec((None, tq, LANES), lambda b, h, i: (b, i, h)),
                pl.BlockSpec((None, sk_len, LANES), lambda b, h, i: (b, 0, h // group)),
                pl.BlockSpec((None, sk_len, LANES), lambda b, h, i: (b, 0, h // group))]
    ins = [q, k, v]
    if has_bias:
        bg = hq // bias.shape[0]
        in_specs.append(pl.BlockSpec((None, 3, tq, tk), lambda b, h, i: (h // bg, 0, 0, 0)))
        ins.append(bias)
    return pl.pallas_call(
        functools.partial(_flash_kernel, causal=causal, has_bias=has_bias, tq=tq, tk=tk, nk=sk_len // tk),
        grid=(n, hq, sq_len // tq),
        in_specs=in_specs,
        out_specs=pl.BlockSpec((None, tq, LANES), lambda b, h, i: (b, i, h)),
        out_shape=jax.ShapeDtypeStruct((n, sq_len, hq * LANES), out_dtype),
        scratch_shapes=[pltpu.VMEM((tq, 1), F32), pltpu.VMEM((tq, 1), F32), pltpu.VMEM((tq, LANES), F32)],
        compiler_params=_cparams(("parallel", "parallel", "arbitrary")),
        name=name,
    )(*ins)


def _t5_bucket_bias(table, dist):
    n = jnp.maximum(dist, 0)
    nf = jnp.maximum(n, 1).astype(F32)
    large = MAX_EXACT + (jnp.log(nf / MAX_EXACT) / math.log(MAX_DISTANCE / MAX_EXACT)
                         * (N_BUCKETS - MAX_EXACT)).astype(jnp.int32)
    bucket = jnp.where(n < MAX_EXACT, n, jnp.minimum(large, N_BUCKETS - 1))
    return jnp.moveaxis(table.astype(F32)[bucket], -1, 0)


def _diff_bias_tiles(table, t):
    assert t >= MAX_DISTANCE
    r = jnp.arange(t)[:, None]
    c = jnp.arange(t)[None, :]
    tiles = []
    for cls in range(3):
        d = cls * t + r - c
        b = _t5_bucket_bias(table, d)
        if cls == 0:
            b = jnp.where(d >= 0, b, NEG_BIG)
        tiles.append(b)
    return jnp.stack(tiles, axis=1)


def _memkv_kernel(m_ref, w_ref, gmem_ref, gk_ref, k_o, v_o, kb_o, vb_o):
    x = m_ref[...]
    r = _rsqrt_mean(jnp.sum(x * x, axis=-1, keepdims=True), D_MODEL)
    h = (x * r * gmem_ref[...]).astype(BF)
    kv = jnp.dot(h, w_ref[...], preferred_element_type=F32)
    gk = gk_ref[...]
    nk = MEM_HEADS * MEM_HD
    for hh in range(MEM_HEADS):
        z = kv[:, hh * LANES:(hh + 1) * LANES]
        rr = _rsqrt_mean(jnp.sum(z * z, axis=-1, keepdims=True), MEM_HD)
        y = z * rr * gk
        k_o[:, hh * LANES:(hh + 1) * LANES] = y
        kb_o[:, hh * LANES:(hh + 1) * LANES] = y.astype(BF)
    v = kv[:, nk:]
    v_o[...] = v
    vb_o[...] = v.astype(BF)


def _memory_kv(mem2d, w_mem_kv, g_mem, g_k_mem, tm):
    t = mem2d.shape[0]
    nk = MEM_HEADS * MEM_HD
    row = lambda w: pl.BlockSpec((tm, w), lambda i: (i, 0))
    w = w_mem_kv.astype(BF)
    return pl.pallas_call(
        _memkv_kernel,
        grid=(t // tm,),
        in_specs=[row(D_MODEL), _const_spec(w.shape), _const_spec((1, D_MODEL)), _const_spec((1, MEM_HD))],
        out_specs=[row(nk)] * 4,
        out_shape=[jax.ShapeDtypeStruct((t, nk), F32)] * 2 + [jax.ShapeDtypeStruct((t, nk), BF)] * 2,
        compiler_params=_cparams(("parallel",)),
        name="mem_kv",
    )(mem2d, w, g_mem.reshape(1, -1), g_k_mem.reshape(1, -1))


def _merge_kernel(lam_ref, x_ref, omla_ref, odiff_ref, omem_ref, wg_ref, wmla_ref, wdiff_ref, wmem_ref, wout_ref,
                  gattn_ref, gsub_ref, gffn_ref, wr_ref, br_ref, x2_o, h2_o, topi_o, gatew_o):
    x = x_ref[...]
    r = _rsqrt_mean(jnp.sum(x * x, axis=-1, keepdims=True), D_MODEL)
    h = (x * r * gattn_ref[...]).astype(BF)
    lam = lam_ref[0]

    def gate(b):
        return jax.nn.sigmoid(jnp.dot(h, wg_ref[:, b * D_MODEL:(b + 1) * D_MODEL], preferred_element_type=F32))

    mixed = gate(0) * jnp.dot(omla_ref[...], wmla_ref[...], preferred_element_type=F32)

    od = odiff_ref[...]
    gsub = gsub_ref[...]
    bdiff = None
    for hh in range(DIFF_HEADS):
        o = od[:, (2 * hh) * LANES:(2 * hh + 1) * LANES] - lam * od[:, (2 * hh + 1) * LANES:(2 * hh + 2) * LANES]
        rr = _rsqrt_mean(jnp.sum(o * o, axis=-1, keepdims=True), DIFF_VD)
        o = (o * rr * gsub * (1.0 - LAM_INIT)).astype(BF)
        part = jnp.dot(o, wdiff_ref[hh * LANES:(hh + 1) * LANES, :], preferred_element_type=F32)
        bdiff = part if bdiff is None else bdiff + part
    mixed = mixed + gate(1) * bdiff
    mixed = mixed + gate(2) * jnp.dot(omem_ref[...], wmem_ref[...], preferred_element_type=F32)
    x2 = x + jnp.dot(mixed.astype(BF), wout_ref[...], preferred_element_type=F32)
    x2_o[...] = x2

    r2 = _rsqrt_mean(jnp.sum(x2 * x2, axis=-1, keepdims=True), D_MODEL)
    h2 = x2 * r2 * gffn_ref[...]
    h2_o[...] = h2
    logits = jnp.dot(h2, wr_ref[...], preferred_element_type=F32, precision=lax.Precision.HIGHEST) + br_ref[...]
    lane = lax.broadcasted_iota(jnp.int32, logits.shape, 1)
    vals, idxs = [], []
    l = logits
    for _ in range(TOP_K):
        mx = jnp.max(l, axis=-1, keepdims=True)
        idx = jnp.min(jnp.where(l == mx, lane, LANES), axis=-1, keepdims=True)
        vals.append(mx)
        idxs.append(idx)
        l = jnp.where(lane == idx, -jnp.inf, l)
    es = [jnp.exp(v - vals[0]) for v in vals]
    tot = es[0] + es[1] + es[2] + es[3]
    ti = jnp.zeros(logits.shape, jnp.int32)
    gw = jnp.zeros(logits.shape, F32)
    for k in range(TOP_K):
        ti = jnp.where(lane == k, idxs[k], ti)
        gw = jnp.where(lane == k, es[k] / tot, gw)
    topi_o[...] = ti
    gatew_o[...] = gw


def _merge(x2d, o_mla, o_diff, o_mem, lam, mw, tm):
    t = x2d.shape[0]
    row = lambda w: pl.BlockSpec((tm, w), lambda i: (i, 0))
    wnames = ["wg", "wmla", "wdiff", "wmem", "wout", "gattn", "gsub", "gffn", "wr", "br"]
    ws = [mw[k] for k in wnames]
    return pl.pallas_call(
        _merge_kernel,
        grid=(t // tm,),
        in_specs=[pl.BlockSpec(memory_space=pltpu.SMEM), row(D_MODEL), row(o_mla.shape[1]), row(o_diff.shape[1]),
                  row(o_mem.shape[1])] + [_const_spec(w.shape) for w in ws],
        out_specs=[row(D_MODEL), row(D_MODEL), row(LANES), row(LANES)],
        out_shape=[jax.ShapeDtypeStruct((t, D_MODEL), F32), jax.ShapeDtypeStruct((t, D_MODEL), F32),
                   jax.ShapeDtypeStruct((t, LANES), jnp.int32), jax.ShapeDtypeStruct((t, LANES), F32)],
        compiler_params=_cparams(("parallel",)),
        name="merge",
    )(lam, x2d, o_mla, o_diff, o_mem, *ws)


def _dispatch_kernel(pend_ref, dest_ref, h_ref, xs_hbm, zero_sc, sem, *, tm, tmg):
    i = pl.program_id(0)
    n_tiles = xs_hbm.shape[0] // tmg

    @pl.when(i == 0)
    def _():
        zero_sc[...] = jnp.zeros(zero_sc.shape, F32)

        def tail(e):
            end = pend_ref[e + 1]
            return pl.multiple_of(end - tmg, tmg), end > pend_ref[e]

        for e in range(N_EXPERTS):
            start, nonempty = tail(e)

            @pl.when(nonempty)
            def _():
                pltpu.make_async_copy(zero_sc, xs_hbm.at[pl.ds(start, tmg)], sem).start()
        for e in range(N_EXPERTS):
            start, nonempty = tail(e)

            @pl.when(nonempty)
            def _():
                pltpu.make_async_copy(zero_sc, xs_hbm.at[pl.ds(start, tmg)], sem).wait()

        def unused(tile):
            return pltpu.make_async_copy(zero_sc, xs_hbm.at[pl.ds(pl.multiple_of(tile * tmg, tmg), tmg)], sem)

        first_unused = pend_ref[N_EXPERTS] // tmg
        lax.fori_loop(first_unused, n_tiles, lambda tile, c: (unused(tile).start(), c)[1], 0)
        lax.fori_loop(first_unused, n_tiles, lambda tile, c: (unused(tile).wait(), c)[1], 0)

    def issue(rk, c):
        r = rk // TOP_K
        pltpu.make_async_copy(h_ref.at[pl.ds(r, 1)], xs_hbm.at[pl.ds(dest_ref[0, 0, rk], 1)], sem).start()
        return c

    lax.fori_loop(0, tm * TOP_K, issue, 0)
    for _ in range(TOP_K):
        pltpu.make_async_copy(h_ref, xs_hbm.at[pl.ds(0, tm)], sem).wait()


def _dispatch(h2, dest, pend0, n_rows, tm, tmg):
    t = h2.shape[0]
    dest3 = dest.reshape(t // tm, 1, tm * TOP_K)
    return pl.pallas_call(
        functools.partial(_dispatch_kernel, tm=tm, tmg=tmg),
        grid_spec=pltpu.PrefetchScalarGridSpec(
            num_scalar_prefetch=1,
            grid=(t // tm,),
            in_specs=[pl.BlockSpec((1, 1, tm * TOP_K), lambda i, p: (i, 0, 0), memory_space=pltpu.SMEM),
                      pl.BlockSpec((tm, D_MODEL), lambda i, p: (i, 0))],
            out_specs=pl.BlockSpec(memory_space=pl.ANY),
            scratch_shapes=[pltpu.VMEM((tmg, D_MODEL), F32), pltpu.SemaphoreType.DMA(())],
        ),
        out_shape=jax.ShapeDtypeStruct((n_rows, D_MODEL), F32),
        compiler_params=_cparams(("arbitrary",)),
        name="moe_dispatch",
    )(pend0, dest3, h2)


def _experts_kernel(te_ref, nu_ref, x_ref, wgu_ref, bgu_ref, wd_ref, bd_ref, o_ref):
    i = pl.program_id(0)

    @pl.when(i < nu_ref[0])
    def _():
        x = x_ref[...].astype(BF)
        gu = jnp.dot(x, wgu_ref[...], preferred_element_type=F32) + bgu_ref[...]
        gate = jnp.minimum(gu[:, :D_EXPERT], SWIGLU_LIMIT)
        up = jnp.clip(gu[:, D_EXPERT:], -SWIGLU_LIMIT, SWIGLU_LIMIT)
        act = (up + 1.0) * (gate * jax.nn.sigmoid(SWIGLU_ALPHA * gate))
        o_ref[...] = jnp.dot(act.astype(BF), wd_ref[...], preferred_element_type=F32) + bd_ref[...]

    @pl.when(i >= nu_ref[0])
    def _():
        o_ref[...] = jnp.zeros(o_ref.shape, F32)


def _experts(xs, tile_expert, n_used, wgu, bgu, wd, bd, tmg):
    n_rows = xs.shape[0]
    n_tiles = n_rows // tmg
    return pl.pallas_call(
        _experts_kernel,
        grid_spec=pltpu.PrefetchScalarGridSpec(
            num_scalar_prefetch=2,
            grid=(n_tiles,),
            in_specs=[pl.BlockSpec((tmg, D_MODEL), lambda i, te, nu: (jnp.minimum(i, nu[0] - 1), 0)),
                      pl.BlockSpec((None, D_MODEL, 2 * D_EXPERT), lambda i, te, nu: (te[i], 0, 0)),
                      pl.BlockSpec((None, 1, 2 * D_EXPERT), lambda i, te, nu: (te[i], 0, 0)),
                      pl.BlockSpec((None, D_EXPERT, D_MODEL), lambda i, te, nu: (te[i], 0, 0)),
                      pl.BlockSpec((None, 1, D_MODEL), lambda i, te, nu: (te[i], 0, 0))],
            out_specs=pl.BlockSpec((tmg, D_MODEL), lambda i, te, nu: (i, 0)),
        ),
        out_shape=jax.ShapeDtypeStruct((n_rows, D_MODEL), F32),
        compiler_params=_cparams(("arbitrary",)),
        name="moe_experts",
    )(tile_expert, n_used, xs, wgu, bgu, wd, bd)


def _combine_kernel(dest_ref, x2_ref, gw_ref, out_hbm, y_ref, buf, sem, *, tm):
    def issue(rk, c):
        r = rk // TOP_K
        k = rk % TOP_K
        pltpu.make_async_copy(out_hbm.at[pl.ds(dest_ref[0, 0, rk], 1)], buf.at[k, pl.ds(r, 1)], sem).start()
        return c

    lax.fori_loop(0, tm * TOP_K, issue, 0)
    for k in range(TOP_K):
        pltpu.make_async_copy(out_hbm.at[pl.ds(0, tm)], buf.at[k], sem).wait()
    gw = gw_ref[...]
    y = x2_ref[...]
    for k in range(TOP_K):
        y = y + gw[:, k:k + 1] * buf[k]
    y_ref[...] = y


def _combine(x2, gate_w, dest, out_rows, tm):
    t = x2.shape[0]
    dest3 = dest.reshape(t // tm, 1, tm * TOP_K)
    return pl.pallas_call(
        functools.partial(_combine_kernel, tm=tm),
        grid=(t // tm,),
        in_specs=[pl.BlockSpec((1, 1, tm * TOP_K), lambda i: (i, 0, 0), memory_space=pltpu.SMEM),
                  pl.BlockSpec((tm, D_MODEL), lambda i: (i, 0)),
                  pl.BlockSpec((tm, LANES), lambda i: (i, 0)),
                  pl.BlockSpec(memory_space=pl.ANY)],
        out_specs=pl.BlockSpec((tm, D_MODEL), lambda i: (i, 0)),
        out_shape=jax.ShapeDtypeStruct((t, D_MODEL), F32),
        scratch_shapes=[pltpu.VMEM((TOP_K, tm, D_MODEL), F32), pltpu.SemaphoreType.DMA(())],
        compiler_params=_cparams(("arbitrary",)),
        name="moe_combine",
    )(dest3, x2, gate_w, out_rows)


def _route(topi, tmg, n_tiles):
    e_ids = jnp.arange(N_EXPERTS, dtype=jnp.int32)
    hit = (topi[:, :, None] == e_ids[None, None, :])
    onehot = jnp.any(hit, axis=1).astype(jnp.int32)
    counts = jnp.sum(onehot, axis=0)
    pos = jnp.cumsum(onehot, axis=0) - onehot
    padded = (counts + tmg - 1) // tmg * tmg
    pend = jnp.cumsum(padded)
    pstart = pend - padded
    base = pstart[None, :] + pos
    dest = jnp.sum(jnp.where(hit, base[:, None, :], 0), axis=-1).astype(jnp.int32)
    n_used = (pend[-1] // tmg).astype(jnp.int32).reshape(1)
    tile_expert = jnp.minimum(jnp.searchsorted(pend, jnp.arange(n_tiles) * tmg, side='right'),
                              N_EXPERTS - 1).astype(jnp.int32)
    pend0 = jnp.concatenate([jnp.zeros((1,), jnp.int32), pend.astype(jnp.int32)])
    return dest, pend0, n_used, tile_expert


def _moe(x2, h2, topi_pad, gate_w, ew, tm_tok, tmg):
    t = x2.shape[0]
    n_tiles = (t * TOP_K + N_EXPERTS * (tmg - 1) + tmg - 1) // tmg
    dest, pend0, n_used, tile_expert = _route(topi_pad[:, :TOP_K], tmg, n_tiles)
    xs = _dispatch(h2, dest.reshape(-1), pend0, n_tiles * tmg, tm_tok, tmg)
    out_rows = _experts(xs, tile_expert, n_used, ew["wgu"], ew["bgu"], ew["wd"], ew["bd"], tmg)
    return _combine(x2, gate_w, dest.reshape(-1), out_rows, tm_tok)


def _absorb_kernel(qm_ref, wukt_ref, o_ref):
    for hh in range(MLA_HEADS):
        qn = qm_ref[:, hh * LANES:hh * LANES + MLA_NOPE]
        o_ref[hh] = jnp.dot(qn, wukt_ref[hh], preferred_element_type=F32).astype(BF)


def _absorb(qm, wukt):
    ns = qm.shape[0]
    return pl.pallas_call(
        _absorb_kernel,
        in_specs=[_const_spec(qm.shape), _const_spec(wukt.shape)],
        out_specs=_const_spec((MLA_HEADS, ns, KV_RANK)),
        out_shape=jax.ShapeDtypeStruct((MLA_HEADS, ns, KV_RANK), BF),
        grid=(1,),
        compiler_params=_cparams(("arbitrary",)),
        name="sample_q_absorb",
    )(qm, wukt)


def _paged_kernel(pt_ref, qlat_ref, qr_ref, qbd_ref, bias_ref, *rest, pb):
    lat_refs = rest[0:pb]
    krt_refs = rest[pb:2 * pb]
    kt_refs = rest[2 * pb:3 * pb]
    v_refs = rest[3 * pb:4 * pb]
    oa_ref, od_ref, ma, la, acca, md, ld, accd = rest[4 * pb:]
    j = pl.program_id(1)

    @pl.when(j == 0)
    def _():
        ma[...] = jnp.full(ma.shape, -jnp.inf, F32)
        la[...] = jnp.zeros(la.shape, F32)
        acca[...] = jnp.zeros(acca.shape, F32)
        md[...] = jnp.full(md.shape, -jnp.inf, F32)
        ld[...] = jnp.zeros(ld.shape, F32)
        accd[...] = jnp.zeros(accd.shape, F32)

    qlat = qlat_ref[...]
    qr = qr_ref[...]
    qbd = qbd_ref[...]
    rowh = lax.broadcasted_iota(jnp.int32, (2 * DIFF_HEADS, LANES), 0) // 2
    for p in range(pb):
        lat = lat_refs[p][...].astype(BF)
        s = lax.dot_general(qlat, lat, (((1,), (1,)), ((), ())), preferred_element_type=F32)
        s = s + jnp.dot(qr, krt_refs[p][...].astype(BF), preferred_element_type=F32)
        m_new = jnp.maximum(ma[...], jnp.max(s, axis=-1, keepdims=True))
        alpha = jnp.exp(ma[...] - m_new)
        pe = jnp.exp(s - m_new)
        la[...] = alpha * la[...] + jnp.sum(pe, axis=-1, keepdims=True)
        acca[...] = alpha * acca[...] + jnp.dot(pe.astype(BF), lat, preferred_element_type=F32)
        ma[...] = m_new
        s = jnp.dot(qbd, kt_refs[p][...].astype(BF), preferred_element_type=F32)
        s = s + bias_ref[:, p * PAGE_SIZE:(p + 1) * PAGE_SIZE]
        m_new = jnp.maximum(md[...], jnp.max(s, axis=-1, keepdims=True))
        alpha = jnp.exp(md[...] - m_new)
        pe = jnp.exp(s - m_new)
        ld[...] = alpha * ld[...] + jnp.sum(pe, axis=-1, keepdims=True)
        peb = pe.astype(BF)
        pv = jnp.zeros((2 * DIFF_HEADS, DIFF_VD), F32)
        for hh in range(DIFF_HEADS):
            vh = v_refs[p][pl.ds(hh, PAGE_SIZE, stride=DIFF_HEADS), :].astype(BF)
            pv = jnp.where(rowh == hh, jnp.dot(peb, vh, preferred_element_type=F32), pv)
        accd[...] = alpha * accd[...] + pv
        md[...] = m_new

    @pl.when(j == pl.num_programs(1) - 1)
    def _():
        oa_ref[:, 0:KV_RANK] = acca[...]
        oa_ref[:, KV_RANK:KV_RANK + LANES] = jnp.broadcast_to(ma[...], (MLA_HEADS, LANES))
        oa_ref[:, KV_RANK + LANES:] = jnp.broadcast_to(la[...], (MLA_HEADS, LANES))
        od_ref[:, 0:DIFF_VD] = accd[...]
        od_ref[:, DIFF_VD:2 * DIFF_VD] = jnp.broadcast_to(md[...], (2 * DIFF_HEADS, LANES))
        od_ref[:, 2 * DIFF_VD:] = jnp.broadcast_to(ld[...], (2 * DIFF_HEADS, LANES))


def _paged_attention(page_table, qlat, qr, qbd, bias, lat_pages, krt_pages, kt_pages, v_pages, pb):
    ns, n_pages = page_table.shape
    assert n_pages % pb == 0

    def page_spec(shape, p):
        return pl.BlockSpec((None,) + shape, lambda n, j, pt: (pt[n, j * pb + p], 0, 0))

    per_n = lambda shape: pl.BlockSpec((None,) + shape, lambda n, j, pt: (n, 0, 0))
    in_specs = [per_n((MLA_HEADS, KV_RANK)), per_n((MLA_HEADS, MLA_ROPE)), per_n((2 * DIFF_HEADS, 512)),
                pl.BlockSpec((2 * DIFF_HEADS, pb * PAGE_SIZE), lambda n, j, pt: (0, j))]
    ins = [qlat, qr, qbd, bias]
    for arr, shape in ((lat_pages, (PAGE_SIZE, KV_RANK)), (krt_pages, (MLA_ROPE, PAGE_SIZE)),
                       (kt_pages, (512, PAGE_SIZE)), (v_pages, (512, DIFF_VD))):
        for p in range(pb):
            in_specs.append(page_spec(shape, p))
            ins.append(arr)
    wa = KV_RANK + 2 * LANES
    wd = 3 * DIFF_VD
    return pl.pallas_call(
        functools.partial(_paged_kernel, pb=pb),
        grid_spec=pltpu.PrefetchScalarGridSpec(
            num_scalar_prefetch=1,
            grid=(ns, n_pages // pb),
            in_specs=in_specs,
            out_specs=[per_n((MLA_HEADS, wa)), per_n((2 * DIFF_HEADS, wd))],
            scratch_shapes=[pltpu.VMEM((MLA_HEADS, 1), F32), pltpu.VMEM((MLA_HEADS, 1), F32),
                            pltpu.VMEM((MLA_HEADS, KV_RANK), F32),
                            pltpu.VMEM((2 * DIFF_HEADS, 1), F32), pltpu.VMEM((2 * DIFF_HEADS, 1), F32),
                            pltpu.VMEM((2 * DIFF_HEADS, DIFF_VD), F32)],
        ),
        out_shape=[jax.ShapeDtypeStruct((ns, MLA_HEADS, wa), F32),
                   jax.ShapeDtypeStruct((ns, 2 * DIFF_HEADS, wd), F32)],
        compiler_params=_cparams(("parallel", "arbitrary")),
        name="sample_paged_attn",
    )(page_table, *ins)


def _finish_kernel(b0_ref, pa_ref, pd_ref, qlat_ref, qm_ref, qd_ref, ckv_ref, kr_ref, kd_ref, vd_ref, wuv_ref,
                   omla_o, odiff_o):
    ckv = ckv_ref[...]
    kr = kr_ref[...]
    krs = pltpu.roll(kr, MLA_NOPE, 1)
    lane = lax.broadcasted_iota(jnp.int32, (1, LANES), 1)
    rope_lanes = (lane >= MLA_NOPE) & (lane < MLA_QK)
    for hh in range(MLA_HEADS):
        acc = pa_ref[hh, :, 0:KV_RANK]
        m = pa_ref[hh, :, KV_RANK:KV_RANK + 1]
        l = pa_ref[hh, :, KV_RANK + LANES:KV_RANK + LANES + 1]
        qg = qm_ref[:, hh * LANES:(hh + 1) * LANES].astype(F32)
        s = (jnp.sum(qlat_ref[hh].astype(F32) * ckv, axis=-1, keepdims=True)
             + jnp.sum(jnp.where(rope_lanes, qg * krs, 0.0), axis=-1, keepdims=True))
        m_new = jnp.maximum(m, s)
        alpha = jnp.exp(m - m_new)
        pn = jnp.exp(s - m_new)
        l = alpha * l + pn
        acc = alpha * acc + pn * ckv
        o_lat = (acc / l).astype(BF)
        omla_o[:, hh * LANES:(hh + 1) * LANES] = jnp.dot(
            o_lat, wuv_ref[:, hh * LANES:(hh + 1) * LANES], preferred_element_type=F32).astype(BF)
    for g in range(2 * DIFF_HEADS):
        hh = g // 2
        acc = pd_ref[g, :, 0:DIFF_VD]
        m = pd_ref[g, :, DIFF_VD:DIFF_VD + 1]
        l = pd_ref[g, :, 2 * DIFF_VD:2 * DIFF_VD + 1]
        qg = qd_ref[:, g * LANES:(g + 1) * LANES].astype(F32)
        s = jnp.sum(qg * kd_ref[:, hh * LANES:(hh + 1) * LANES], axis=-1, keepdims=True) + b0_ref[hh]
        m_new = jnp.maximum(m, s)
        alpha = jnp.exp(m - m_new)
        pn = jnp.exp(s - m_new)
        l = alpha * l + pn
        acc = alpha * acc + pn * vd_ref[:, hh * LANES:(hh + 1) * LANES]
        odiff_o[:, g * LANES:(g + 1) * LANES] = acc / l


def _finish(b0, pa, pd, qlat, qm, qd, ckv, kr128, kd, vd, wuv):
    ns = qm.shape[0]
    ins = [pa, pd, qlat, qm, qd, ckv, kr128, kd, vd, wuv]
    return pl.pallas_call(
        _finish_kernel,
        grid=(1,),
        in_specs=[pl.BlockSpec(memory_space=pltpu.SMEM)] + [_const_spec(a.shape) for a in ins],
        out_specs=[_const_spec((ns, MLA_HEADS * LANES)), _const_spec((ns, 2 * DIFF_HEADS * LANES))],
        out_shape=[jax.ShapeDtypeStruct((ns, MLA_HEADS * LANES), BF),
                   jax.ShapeDtypeStruct((ns, 2 * DIFF_HEADS * LANES), F32)],
        compiler_params=_cparams(("arbitrary",)),
        name="sample_attn_finish",
    )(b0, *ins)


def _mem_decode_kernel(q_ref, k_ref, v_ref, o_ref, *, m_len):
    q = q_ref[...]
    rows = lax.broadcasted_iota(jnp.int32, (8, LANES), 0)
    out = jnp.zeros((8, LANES), F32)
    for hh in range(MEM_HEADS):
        kh = k_ref[pl.ds(hh, m_len, stride=MEM_HEADS), :].astype(BF)
        vh = v_ref[pl.ds(hh, m_len, stride=MEM_HEADS), :].astype(BF)
        s = lax.dot_general(q, kh, (((1,), (1,)), ((), ())), preferred_element_type=F32)
        m = jnp.max(s, axis=-1, keepdims=True)
        pe = jnp.exp(s - m)
        l = jnp.sum(pe, axis=-1, keepdims=True)
        o = jnp.dot(pe.astype(BF), vh, preferred_element_type=F32) / l
        out = jnp.where(rows == hh, o, out)
    o_ref[...] = out.astype(o_ref.dtype)


def _mem_decode(q8, mem_k, mem_v, m_len):
    ns = q8.shape[0]
    blk = lambda r: pl.BlockSpec((None, r, LANES), lambda n: (n, 0, 0))
    return pl.pallas_call(
        functools.partial(_mem_decode_kernel, m_len=m_len),
        grid=(ns,),
        in_specs=[blk(8), blk(m_len * MEM_HEADS), blk(m_len * MEM_HEADS)],
        out_specs=blk(8),
        out_shape=jax.ShapeDtypeStruct((ns, 8, LANES), BF),
        compiler_params=_cparams(("parallel",)),
        name="sample_mem_attn",
    )(q8, mem_k, mem_v)


def _sample_path(x_sample, caches, page_table, t5_bias, pw, lam, w_uk, w_uv):
    cache_lat, cache_kr, cache_dk, cache_dv, cache_mk, cache_mv = caches
    ns, t1, d = x_sample.shape
    assert t1 == 1, "one new token per sample"
    n_pages = page_table.shape[1]
    pos = jnp.full((ns,), n_pages * PAGE_SIZE, jnp.int32)
    (qm, ckv, kr, qd, kd, kdb, vd, vdb, qmem) = _project(
        x_sample.reshape(ns, d), _rope_lane_tables(pos), pw["wp"], pw["gains"], ns, False)
    n_pool = cache_lat.shape[1]
    lat_pages = cache_lat[0]
    krt_pages = jnp.transpose(cache_kr[0], (0, 2, 1))
    kt_pages = jnp.transpose(cache_dk[0], (0, 2, 3, 4, 1)).reshape(n_pool, 512, PAGE_SIZE)
    v_pages = cache_dv[0].reshape(n_pool, PAGE_SIZE * DIFF_HEADS, DIFF_VD)
    m_len = cache_mk.shape[2]
    mem_k = cache_mk[0].reshape(ns, m_len * MEM_HEADS, MEM_HD)
    mem_v = cache_mv[0].reshape(ns, m_len * MEM_HEADS, MEM_HD)

    wukt = jnp.transpose(w_uk, (1, 2, 0)).astype(BF)
    qlat_h = _absorb(qm, wukt)
    qlat = jnp.transpose(qlat_h, (1, 0, 2))
    qr = qm.reshape(ns, MLA_HEADS, LANES)[:, :, MLA_NOPE:MLA_QK]
    qd4 = qd.reshape(ns, DIFF_HEADS, 2, LANES)
    z = jnp.zeros_like(qd4)
    qbd = jnp.stack([jnp.where(jnp.arange(DIFF_HEADS)[None, :, None, None] == hh, qd4, z)
                     for hh in range(DIFF_HEADS)], axis=3)
    qbd = qbd.reshape(ns, 2 * DIFF_HEADS, DIFF_HEADS * LANES)
    q_pos = n_pages * PAGE_SIZE
    bias = _t5_bucket_bias(t5_bias, q_pos - jnp.arange(n_pages * PAGE_SIZE))
    bias = jnp.repeat(bias, 2, axis=0)
    b0 = _t5_bucket_bias(t5_bias, jnp.zeros((1,), jnp.int32))[:, 0]
    pa, pd = _paged_attention(page_table, qlat, qr, qbd, bias, lat_pages, krt_pages, kt_pages, v_pages, pw["pb"])
    kr128 = jnp.pad(kr, ((0, 0), (0, LANES - MLA_ROPE)))
    o_mla, o_diff = _finish(b0, jnp.transpose(pa, (1, 0, 2)), jnp.transpose(pd, (1, 0, 2)), qlat_h, qm, qd, ckv,
                            kr128, kd, vd, pw["wuv"])
    q8 = jnp.pad(qmem.reshape(ns, MEM_HEADS, MEM_HD), ((0, 0), (0, 8 - MEM_HEADS), (0, 0)))
    o_mem = _mem_decode(q8, mem_k, mem_v, m_len)[:, :MEM_HEADS].reshape(ns, MEM_HEADS * MEM_HD)
    x2, h2, topi, gw = _merge(x_sample.reshape(ns, d), o_mla, o_diff, o_mem, lam, pw["merge"], ns)
    y = _moe(x2, h2, topi, gw, pw["experts"], ns, pw["tmg_sample"])
    return (y.reshape(ns, 1, d), ckv.reshape(1, ns, 1, KV_RANK), kr.reshape(1, ns, 1, MLA_ROPE),
            kd.reshape(1, ns, 1, DIFF_HEADS, 2, DIFF_HD), vd.reshape(1, ns, 1, DIFF_HEADS, DIFF_VD))


def _prompt_path(x_prompt, mem_prompt, t5_bias, pw, lam):
    n, s, d = x_prompt.shape
    t = n * s
    pos_tab = _rope_lane_tables(jnp.arange(s))
    (qm, ckv, kr, qd, kd, kdb, vd, vdb, qmem, kmla, vmla) = _project(
        x_prompt.reshape(t, d), pos_tab, pw["wp"], pw["gains"], pw["tm_proj"], True, pw["wuk"], pw["wuv"])
    tq = pw["tq"]
    r3 = lambda a: a.reshape(n, s, a.shape[-1])
    o_mla = _flash(r3(qm), r3(kmla), r3(vmla), None, causal=True, group=1, tq=tq, tk=tq, out_dtype=BF,
                   name="attn_mla")
    o_diff = _flash(r3(qd), r3(kdb), r3(vdb), _diff_bias_tiles(t5_bias, tq), causal=True, group=2, tq=tq, tk=tq,
                    out_dtype=F32, name="attn_diff")
    m_len = mem_prompt.shape[1]
    mk, mv, mkb, mvb = _memory_kv(mem_prompt.reshape(n * m_len, d), pw["w_mem_kv"], pw["g_mem"], pw["g_k_mem"],
                                  min(512, n * m_len))
    rm = lambda a: a.reshape(n, m_len, a.shape[-1])
    o_mem = _flash(r3(qmem), rm(mkb), rm(mvb), None, causal=False, group=1, tq=tq, tk=m_len, out_dtype=BF,
                   name="attn_mem")
    x2, h2, topi, gw = _merge(x_prompt.reshape(t, d), o_mla.reshape(t, -1), o_diff.reshape(t, -1),
                              o_mem.reshape(t, -1), lam, pw["merge"], pw["tm_merge"])
    y = _moe(x2, h2, topi, gw, pw["experts"], pw["tm_tok"], pw["tmg"])
    outs = (y.reshape(n, s, d), ckv.reshape(1, n, s, KV_RANK), kr.reshape(1, n, s, MLA_ROPE),
            kd.reshape(1, n, s, DIFF_HEADS, 2, DIFF_HD), vd.reshape(1, n, s, DIFF_HEADS, DIFF_VD),
            mk.reshape(1, n, m_len, MEM_HEADS, MEM_HD), mv.reshape(1, n, m_len, MEM_HEADS, MEM_HD))
    return outs


def _prepare(g_attn, w_in, g_q_mla, g_ckv, g_krope, w_uk, w_uv, g_q_diff, g_k_diff, g_subln, g_mem, w_mem_kv,
             g_q_mem, g_k_mem, w_br_mla, w_br_diff, w_br_mem, w_out, g_ffn, w_router, b_router, w_gate_up,
             b_gate_up, w_down, b_down):
    wp, wg = _pack_proj_weights(w_in, g_q_mla, g_krope)
    gains = [g_attn.reshape(1, -1),
             _pad_lanes(g_q_mla * (MLA_QK ** -0.5)),
             g_ckv.reshape(1, -1),
             _pad_lanes(g_krope),
             jnp.tile(g_q_diff * (DIFF_HD ** -0.5), 2).reshape(1, -1),
             jnp.tile(g_k_diff, 2).reshape(1, -1),
             (g_q_mem * (MEM_HD ** -0.5)).reshape(1, -1)]
    wuk = jnp.pad(w_uk, ((0, 0), (0, 0), (0, LANES - MLA_NOPE))).reshape(KV_RANK, MLA_HEADS * LANES).astype(BF)
    wuv = jnp.pad(w_uv, ((0, 0), (0, 0), (0, LANES - MLA_VD))).reshape(KV_RANK, MLA_HEADS * LANES).astype(BF)
    wmla = jnp.pad(w_br_mla.reshape(MLA_HEADS, MLA_VD, D_MODEL), ((0, 0), (0, LANES - MLA_VD), (0, 0)))
    wmla = wmla.reshape(MLA_HEADS * LANES, D_MODEL).astype(BF)
    wr = jnp.pad(w_router, ((0, 0), (0, LANES - N_EXPERTS)))
    br = jnp.concatenate([b_router, jnp.full((LANES - N_EXPERTS,), NEG_BIG, F32)]).reshape(1, LANES)
    merge = dict(wg=wg, wmla=wmla, wdiff=w_br_diff.astype(BF), wmem=w_br_mem.astype(BF), wout=w_out.astype(BF),
                 gattn=g_attn.reshape(1, -1), gsub=g_subln.reshape(1, -1), gffn=g_ffn.reshape(1, -1), wr=wr, br=br)
    experts = dict(wgu=w_gate_up.astype(BF), bgu=b_gate_up.reshape(N_EXPERTS, 1, -1), wd=w_down.astype(BF),
                   bd=b_down.reshape(N_EXPERTS, 1, -1))
    return dict(wp=wp, gains=gains, wuk=wuk, wuv=wuv, merge=merge, experts=experts, w_mem_kv=w_mem_kv,
                g_mem=g_mem, g_k_mem=g_k_mem)


def kernel(x_prompt, x_sample, mem_prompt, cache_mla_latent, cache_mla_krope, cache_diff_k, cache_diff_v,
           cache_mem_k, cache_mem_v, page_table, t5_bias, g_attn, w_in, g_q_mla, g_ckv, g_krope, w_uk, w_uv,
           g_q_diff, g_k_diff, lambda_q1, lambda_k1, lambda_q2, lambda_k2, g_subln, g_mem, w_mem_kv, g_q_mem,
           g_k_mem, w_br_mla, w_br_diff, w_br_mem, w_out, g_ffn, w_router, b_router, w_gate_up, b_gate_up,
           w_down, b_down):
    assert g_attn.shape[0] == 1, "single-layer trunk"
    l = 0
    pw = _prepare(g_attn[l], w_in[l], g_q_mla[l], g_ckv[l], g_krope[l], w_uk[l], w_uv[l], g_q_diff[l], g_k_diff[l],
                  g_subln[l], g_mem[l], w_mem_kv[l], g_q_mem[l], g_k_mem[l], w_br_mla[l], w_br_diff[l],
                  w_br_mem[l], w_out[l], g_ffn[l], w_router[l], b_router[l], w_gate_up[l], b_gate_up[l],
                  w_down[l], b_down[l])
    lam = (jnp.exp(jnp.sum(lambda_q1[l] * lambda_k1[l]).astype(F32))
           - jnp.exp(jnp.sum(lambda_q2[l] * lambda_k2[l]).astype(F32)) + LAM_INIT).reshape(1)
    s = x_prompt.shape[1]
    pw.update(tm_proj=min(TILE_PROJ, s), tq=min(TILE_ATTN, s), tm_merge=min(TILE_MERGE, s),
              tm_tok=min(TILE_TOKEN, s), tmg=TILE_EXPERT, tmg_sample=TILE_EXPERT_SAMPLE, pb=PAGES_PER_STEP)
    p = _prompt_path(x_prompt, mem_prompt, t5_bias, pw, lam)
    caches = (cache_mla_latent, cache_mla_krope, cache_diff_k, cache_diff_v, cache_mem_k, cache_mem_v)
    sm = _sample_path(x_sample, caches, page_table, t5_bias, pw, lam, w_uk[l], w_uv[l])
    return (p[0], sm[0]) + p[1:] + sm[1:]
```

```python
import functools
import math

import jax
import jax.numpy as jnp
import numpy as np
from jax import lax
from jax.experimental import pallas as pl
from jax.experimental.pallas import tpu as pltpu

D_MODEL = 1024
PAGE_SIZE = 128
MLA_HEADS = 8
MLA_NOPE = 64
MLA_ROPE = 32
MLA_QK = MLA_NOPE + MLA_ROPE
MLA_VD = 64
KV_RANK = 256
ROPE_THETA = 10000.0
DIFF_HEADS = 4
DIFF_HD = 64
DIFF_VD = 2 * DIFF_HD
MEM_HEADS = 4
MEM_HD = 128
N_BUCKETS = 32
MAX_EXACT = N_BUCKETS // 2
MAX_DISTANCE = 128
N_EXPERTS = 32
TOP_K = 4
D_EXPERT = D_MODEL
SWIGLU_LIMIT = 7.0
SWIGLU_ALPHA = 1.702
EPS = 1e-6
N_BRANCHES = 3
LAM_INIT = 0.8 - 0.6 * math.exp(-0.3 * 0)

LANES = 128
VMEM_LIMIT = 56 * 1024 * 1024
NEG_BIG = -1e30
TILE_PROJ = 256
TILE_ATTN = 512
TILE_MERGE = 512
TILE_TOKEN = 512
ROW_DMA_UNROLL = 4
TILE_EXPERT = 512
TILE_EXPERT_SAMPLE = 128
PAGES_PER_STEP = 8

C_QM = 0
C_CKV = C_QM + MLA_HEADS * LANES
C_KR = C_CKV + KV_RANK
C_QD = C_KR + LANES
C_KD = C_QD + 512
C_VD = C_KD + 512
C_QMEM = C_VD + 512
C_END = C_QMEM + 512

BF = jnp.bfloat16
F32 = jnp.float32


def _cparams(sem):
    return pltpu.CompilerParams(dimension_semantics=sem, vmem_limit_bytes=VMEM_LIMIT)


def _const_spec(shape):
    nd = len(shape)
    return pl.BlockSpec(shape, lambda *a: (0,) * nd)


def _rsqrt_mean(sumsq, n):
    return lax.rsqrt(sumsq * (1.0 / n) + EPS)


def _proj_kernel(x_ref, w_ref, gattn_ref, gq_ref, gckv_ref, gkr_ref, gqd_ref, gkd_ref, gqm_ref,
                 cq_ref, sq_ref, ck_ref, sk_ref, *rest, prompt):
    if prompt:
        (wuk_ref, wuv_ref, qm_o, ckv_o, kr_o, qd_o, kd_o, kdb_o, vd_o, vdb_o, qmem_o, kmla_o, vmla_o) = rest
    else:
        (qm_o, ckv_o, kr_o, qd_o, kd_o, kdb_o, vd_o, vdb_o, qmem_o) = rest
    x = x_ref[...]
    r = _rsqrt_mean(jnp.sum(x * x, axis=-1, keepdims=True), D_MODEL)
    h = (x * r * gattn_ref[...]).astype(BF)
    lane = lax.broadcasted_iota(jnp.int32, (1, LANES), 1)
    lo64 = lane < 64

    def seg(c0, n):
        return jnp.dot(h, w_ref[:, c0:c0 + n], preferred_element_type=F32)

    zq = seg(C_QM, MLA_HEADS * LANES)
    cq, sq = cq_ref[...], sq_ref[...]
    gq = gq_ref[...]
    for g in range(MLA_HEADS):
        z = zq[:, g * LANES:(g + 1) * LANES]
        zm = jnp.where(lane < MLA_QK, z, 0.0)
        rr = _rsqrt_mean(jnp.sum(zm * zm, axis=-1, keepdims=True), MLA_QK)
        y = zm * rr * gq
        yrot = pltpu.roll(z, LANES - MLA_ROPE, 1) * rr
        qm_o[:, g * LANES:(g + 1) * LANES] = (y * cq + yrot * sq).astype(BF)

    zc = seg(C_CKV, KV_RANK)
    rr = _rsqrt_mean(jnp.sum(zc * zc, axis=-1, keepdims=True), KV_RANK)
    ckv = zc * rr * gckv_ref[...]
    ckv_o[...] = ckv

    zk = seg(C_KR, LANES)
    zm = jnp.where(lane < MLA_ROPE, zk, 0.0)
    rr = _rsqrt_mean(jnp.sum(zm * zm, axis=-1, keepdims=True), MLA_ROPE)
    kr = zm * rr * gkr_ref[...] * ck_ref[...] + pltpu.roll(zk, LANES - MLA_ROPE, 1) * rr * sk_ref[...]
    if prompt:
        kr_o[...] = kr.T[:MLA_ROPE, :]
    else:
        kr_o[...] = kr[:, :MLA_ROPE]

    def halfnorm(z, gain):
        sq_ = z * z
        s_lo = jnp.sum(jnp.where(lo64, sq_, 0.0), axis=-1, keepdims=True)
        s_hi = jnp.sum(jnp.where(lo64, 0.0, sq_), axis=-1, keepdims=True)
        rr_ = jnp.where(lo64, _rsqrt_mean(s_lo, DIFF_HD), _rsqrt_mean(s_hi, DIFF_HD))
        return z * rr_ * gain

    zqd = seg(C_QD, 512)
    gqd = gqd_ref[...]
    for hh in range(DIFF_HEADS):
        y = halfnorm(zqd[:, hh * LANES:(hh + 1) * LANES], gqd)
        qd_o[:, (2 * hh) * LANES:(2 * hh + 1) * LANES] = jnp.where(lo64, y, 0.0).astype(BF)
        qd_o[:, (2 * hh + 1) * LANES:(2 * hh + 2) * LANES] = jnp.where(lo64, 0.0, y).astype(BF)

    zkd = seg(C_KD, 512)
    gkd = gkd_ref[...]
    for hh in range(DIFF_HEADS):
        y = halfnorm(zkd[:, hh * LANES:(hh + 1) * LANES], gkd)
        if prompt:
            kd_o[hh * LANES:(hh + 1) * LANES, :] = y.T
        else:
            kd_o[:, hh * LANES:(hh + 1) * LANES] = y
        kdb_o[:, hh * LANES:(hh + 1) * LANES] = y.astype(BF)

    zv = seg(C_VD, 512)
    tm = zv.shape[0]
    for hh in range(DIFF_HEADS):
        vd_o[pl.ds(hh, tm, stride=DIFF_HEADS), :] = zv[:, hh * LANES:(hh + 1) * LANES]
    vdb_o[...] = zv.astype(BF)

    zm_ = seg(C_QMEM, 512)
    gqm = gqm_ref[...]
    for hh in range(MEM_HEADS):
        z = zm_[:, hh * LANES:(hh + 1) * LANES]
        rr = _rsqrt_mean(jnp.sum(z * z, axis=-1, keepdims=True), MEM_HD)
        qmem_o[:, hh * LANES:(hh + 1) * LANES] = (z * rr * gqm).astype(BF)

    if prompt:
        cb = ckv.astype(BF)
        kn = jnp.dot(cb, wuk_ref[...], preferred_element_type=F32)
        krs = pltpu.roll(kr, MLA_NOPE, 1)
        for g in range(MLA_HEADS):
            kmla_o[:, g * LANES:(g + 1) * LANES] = (kn[:, g * LANES:(g + 1) * LANES] + krs).astype(BF)
        vmla_o[...] = jnp.dot(cb, wuv_ref[...], preferred_element_type=F32).astype(BF)


def _rope_lane_tables(pos):
    inv = ROPE_THETA ** (-jnp.arange(0, MLA_ROPE, 2, dtype=F32) / MLA_ROPE)
    ang = pos.astype(F32)[:, None] * inv[None, :]
    cos, sin = jnp.cos(ang), jnp.sin(ang)
    n = pos.shape[0]
    one = jnp.ones((n, MLA_NOPE), F32)
    z32 = jnp.zeros((n, 32), F32)
    z64 = jnp.zeros((n, 64), F32)
    cq = jnp.concatenate([one, cos, cos, z32], axis=1)
    sq = jnp.concatenate([z64, sin, sin, z32], axis=1)
    ck = jnp.concatenate([cos, cos, z32, z64], axis=1)
    sk = jnp.concatenate([sin, sin, z32, z64], axis=1)
    return cq, sq, ck, sk


def _rot_cols(w, g):
    half = MLA_ROPE // 2
    return jnp.concatenate([-w[..., half:] * g[half:], w[..., :half] * g[:half]], axis=-1)


def _pack_proj_weights(w_in, g_q_mla, g_krope):
    d = w_in.shape[0]
    cuts = np.cumsum([MLA_HEADS * MLA_QK, KV_RANK, MLA_ROPE, 512, 512, 512, 512])
    wq = w_in[:, :cuts[0]].reshape(d, MLA_HEADS, MLA_QK)
    scale = MLA_QK ** -0.5
    rotq = _rot_cols(wq[..., MLA_NOPE:], g_q_mla[MLA_NOPE:] * scale)
    wq = jnp.concatenate([wq, rotq], axis=-1).reshape(d, MLA_HEADS * LANES)
    wc = w_in[:, cuts[0]:cuts[1]]
    wk = w_in[:, cuts[1]:cuts[2]]
    wk = jnp.concatenate([wk, _rot_cols(wk, g_krope), jnp.zeros((d, 64), F32)], axis=-1)
    rest = w_in[:, cuts[2]:cuts[6]]
    wp = jnp.concatenate([wq, wc, wk, rest], axis=1).astype(BF)
    wg = w_in[:, cuts[6]:].astype(BF)
    return wp, wg


def _pad_lanes(v, n=LANES):
    return jnp.pad(v, (0, n - v.shape[0])).reshape(1, n)


def _project(x2d, pos_tab, wp, gains, tm, prompt, wuk=None, wuv=None):
    t = x2d.shape[0]
    cq, sq, ck, sk = pos_tab
    nper = cq.shape[0] // tm
    row = lambda w: pl.BlockSpec((tm, w), lambda i: (i, 0))
    tab = pl.BlockSpec((tm, LANES), lambda i: (i % nper, 0))
    ins = [x2d, wp] + list(gains) + [cq, sq, ck, sk]
    in_specs = ([row(D_MODEL), _const_spec(wp.shape)] + [_const_spec(g.shape) for g in gains] + [tab] * 4)
    rows = lambda w, dt: (jax.ShapeDtypeStruct((t, w), dt), row(w))
    if prompt:
        nb = t // cq.shape[0]
        tr = lambda w: (jax.ShapeDtypeStruct((nb, w, cq.shape[0]), F32),
                        pl.BlockSpec((None, w, tm), lambda i: (i // nper, 0, i % nper)))
        kr_out, kd_out = tr(MLA_ROPE), tr(512)
    else:
        kr_out, kd_out = rows(MLA_ROPE, F32), rows(512, F32)
    vd_out = (jax.ShapeDtypeStruct((t * DIFF_HEADS, DIFF_VD), F32),
              pl.BlockSpec((tm * DIFF_HEADS, DIFF_VD), lambda i: (i, 0)))
    outs = [rows(1024, BF), rows(KV_RANK, F32), kr_out, rows(1024, BF), kd_out, rows(512, BF), vd_out,
            rows(512, BF), rows(512, BF)]
    if prompt:
        ins += [wuk, wuv]
        in_specs += [_const_spec(wuk.shape), _const_spec(wuv.shape)]
        outs += [rows(1024, BF), rows(1024, BF)]
    return pl.pallas_call(
        functools.partial(_proj_kernel, prompt=prompt),
        grid=(t // tm,),
        in_specs=in_specs,
        out_specs=[spec for _, spec in outs],
        out_shape=[shape for shape, _ in outs],
        compiler_params=_cparams(("parallel",)),
        name="proj_prompt" if prompt else "proj_sample",
    )(*ins)


def _flash_kernel(q_ref, k_ref, v_ref, *rest, causal, has_bias, tq, tk, nk, hps, kv_shared):
    if has_bias:
        bias_ref, o_ref, m_sc, l_sc, acc_sc = rest
    else:
        o_ref, m_sc, l_sc, acc_sc = rest
    i = pl.program_id(2)
    m_sc[...] = jnp.full(m_sc.shape, -jnp.inf, F32)
    l_sc[...] = jnp.zeros(l_sc.shape, F32)
    acc_sc[...] = jnp.zeros(acc_sc.shape, F32)

    def step(j, masked):
        off = pl.multiple_of(j * tk, tk)
        scores = []
        for c in range(hps):
            kc = 0 if kv_shared else c
            q = q_ref[:, c * LANES:(c + 1) * LANES]
            k = k_ref[pl.ds(off, tk), kc * LANES:(kc + 1) * LANES]
            s = lax.dot_general(q, k, (((1,), (1,)), ((), ())), preferred_element_type=F32)
            if has_bias:
                s = s + bias_ref[jnp.minimum(i - j, 2)]
            elif masked:
                rows = lax.broadcasted_iota(jnp.int32, (tq, tk), 0)
                cols = lax.broadcasted_iota(jnp.int32, (tq, tk), 1)
                s = jnp.where(rows >= cols, s, NEG_BIG)
            scores.append(s)
        for c in range(hps):
            kc = 0 if kv_shared else c
            s = scores[c]
            v = v_ref[pl.ds(off, tk), kc * LANES:(kc + 1) * LANES]
            m_prev = m_sc[c]
            m_new = jnp.maximum(m_prev, jnp.max(s, axis=-1, keepdims=True))
            alpha = jnp.exp(m_prev - m_new)
            p = jnp.exp(s - jnp.concatenate([m_new] * (tk // LANES), axis=1))
            l_sc[c] = alpha * l_sc[c] + jnp.sum(p, axis=-1, keepdims=True)
            acc_sc[c] = alpha * acc_sc[c] + jnp.dot(p.astype(BF), v, preferred_element_type=F32)
            m_sc[c] = m_new

    if causal:
        def body(j, c):
            step(j, False)
            return c
        lax.fori_loop(0, i, body, 0)
        step(i, True)
    else:
        for j in range(nk):
            step(j, False)
    for c in range(hps):
        o_ref[:, c * LANES:(c + 1) * LANES] = (acc_sc[c] / l_sc[c]).astype(o_ref.dtype)


def _flash(q, k, v, bias, *, causal, kv_shared, hps, tq, tk, out_dtype, name):
    n, sq_len, hq = q.shape[0], q.shape[1], q.shape[2] // LANES
    sk_len = k.shape[1]
    has_bias = bias is not None
    kvw = LANES if kv_shared else hps * LANES
    in_specs = [pl.BlockSpec((None, tq, hps * LANES), lambda b, h, i: (b, i, h)),
                pl.BlockSpec((None, sk_len, kvw), lambda b, h, i: (b, 0, h)),
                pl.BlockSpec((None, sk_len, kvw), lambda b, h, i: (b, 0, h))]
    ins = [q, k, v]
    if has_bias:
        assert kv_shared and bias.shape[0] == hq // hps
        in_specs.append(pl.BlockSpec((None, 3, tq, tk), lambda b, h, i: (h, 0, 0, 0)))
        ins.append(bias)
    return pl.pallas_call(
        functools.partial(_flash_kernel, causal=causal, has_bias=has_bias, tq=tq, tk=tk, nk=sk_len // tk, hps=hps,
                          kv_shared=kv_shared),
        grid=(n, hq // hps, sq_len // tq),
        in_specs=in_specs,
        out_specs=pl.BlockSpec((None, tq, hps * LANES), lambda b, h, i: (b, i, h)),
        out_shape=jax.ShapeDtypeStruct((n, sq_len, hq * LANES), out_dtype),
        scratch_shapes=[pltpu.VMEM((hps, tq, LANES), F32)] * 3,
        compiler_params=_cparams(("parallel", "parallel", "arbitrary")),
        name=name,
    )(*ins)


def _t5_bucket_bias(table, dist):
    n = jnp.maximum(dist, 0)
    nf = jnp.maximum(n, 1).astype(F32)
    large = MAX_EXACT + (jnp.log(nf / MAX_EXACT) / math.log(MAX_DISTANCE / MAX_EXACT)
                         * (N_BUCKETS - MAX_EXACT)).astype(jnp.int32)
    bucket = jnp.where(n < MAX_EXACT, n, jnp.minimum(large, N_BUCKETS - 1))
    tab = table.astype(F32)
    out = jnp.zeros((tab.shape[1],) + bucket.shape, F32)
    for b in range(N_BUCKETS):
        out = out + jnp.where(bucket[None] == b, tab[b].reshape((-1,) + (1,) * bucket.ndim), 0.0)
    return out


def _diff_bias_tiles(table, t):
    assert t >= MAX_DISTANCE
    r = jnp.arange(t)[:, None]
    c = jnp.arange(t)[None, :]
    tiles = []
    for cls in range(3):
        d = cls * t + r - c
        b = _t5_bucket_bias(table, d)
        if cls == 0:
            b = jnp.where(d >= 0, b, NEG_BIG)
        tiles.append(b)
    return jnp.stack(tiles, axis=1)


def _memkv_kernel(m_ref, w_ref, gmem_ref, gk_ref, k_o, v_o, kb_o, vb_o):
    x = m_ref[...]
    r = _rsqrt_mean(jnp.sum(x * x, axis=-1, keepdims=True), D_MODEL)
    h = (x * r * gmem_ref[...]).astype(BF)
    kv = jnp.dot(h, w_ref[...], preferred_element_type=F32)
    gk = gk_ref[...]
    nk = MEM_HEADS * MEM_HD
    tm = x.shape[0]
    v = kv[:, nk:]
    for hh in range(MEM_HEADS):
        z = kv[:, hh * LANES:(hh + 1) * LANES]
        rr = _rsqrt_mean(jnp.sum(z * z, axis=-1, keepdims=True), MEM_HD)
        y = z * rr * gk
        k_o[pl.ds(hh, tm, stride=MEM_HEADS), :] = y
        v_o[pl.ds(hh, tm, stride=MEM_HEADS), :] = v[:, hh * LANES:(hh + 1) * LANES]
        kb_o[:, hh * LANES:(hh + 1) * LANES] = y.astype(BF)
    vb_o[...] = v.astype(BF)


def _memory_kv(mem2d, w_mem_kv, g_mem, g_k_mem, tm):
    t = mem2d.shape[0]
    nk = MEM_HEADS * MEM_HD
    row = lambda w: pl.BlockSpec((tm, w), lambda i: (i, 0))
    w = w_mem_kv.astype(BF)
    return pl.pallas_call(
        _memkv_kernel,
        grid=(t // tm,),
        in_specs=[row(D_MODEL), _const_spec(w.shape), _const_spec((1, D_MODEL)), _const_spec((1, MEM_HD))],
        out_specs=[pl.BlockSpec((tm * MEM_HEADS, MEM_HD), lambda i: (i, 0))] * 2 + [row(nk)] * 2,
        out_shape=[jax.ShapeDtypeStruct((t * MEM_HEADS, MEM_HD), F32)] * 2 + [jax.ShapeDtypeStruct((t, nk), BF)] * 2,
        compiler_params=_cparams(("parallel",)),
        name="mem_kv",
    )(mem2d, w, g_mem.reshape(1, -1), g_k_mem.reshape(1, -1))


def _merge_kernel(lam_ref, x_ref, omla_ref, odiff_ref, omem_ref, wg_ref, wmla_ref, wdiff_ref, wmem_ref, wout_ref,
                  gattn_ref, gsub_ref, gffn_ref, wr_ref, br_ref, x2_o, h2_o, topi_o, gatew_o):
    x = x_ref[...]
    r = _rsqrt_mean(jnp.sum(x * x, axis=-1, keepdims=True), D_MODEL)
    h = (x * r * gattn_ref[...]).astype(BF)
    lam = lam_ref[0]

    def gate(b):
        return jax.nn.sigmoid(jnp.dot(h, wg_ref[:, b * D_MODEL:(b + 1) * D_MODEL], preferred_element_type=F32))

    mixed = gate(0) * jnp.dot(omla_ref[...], wmla_ref[...], preferred_element_type=F32)

    od = odiff_ref[...]
    gsub = gsub_ref[...]
    bdiff = None
    for hh in range(DIFF_HEADS):
        o = od[:, (2 * hh) * LANES:(2 * hh + 1) * LANES] - lam * od[:, (2 * hh + 1) * LANES:(2 * hh + 2) * LANES]
        rr = _rsqrt_mean(jnp.sum(o * o, axis=-1, keepdims=True), DIFF_VD)
        o = (o * rr * gsub * (1.0 - LAM_INIT)).astype(BF)
        part = jnp.dot(o, wdiff_ref[hh * LANES:(hh + 1) * LANES, :], preferred_element_type=F32)
        bdiff = part if bdiff is None else bdiff + part
    mixed = mixed + gate(1) * bdiff
    mixed = mixed + gate(2) * jnp.dot(omem_ref[...], wmem_ref[...], preferred_element_type=F32)
    x2 = x + jnp.dot(mixed.astype(BF), wout_ref[...], preferred_element_type=F32)
    x2_o[...] = x2

    r2 = _rsqrt_mean(jnp.sum(x2 * x2, axis=-1, keepdims=True), D_MODEL)
    h2 = x2 * r2 * gffn_ref[...]
    h2_o[...] = h2
    logits = jnp.dot(h2, wr_ref[...], preferred_element_type=F32, precision=lax.Precision.HIGHEST) + br_ref[...]
    lane = lax.broadcasted_iota(jnp.int32, logits.shape, 1)
    vals, idxs = [], []
    l = logits
    for _ in range(TOP_K):
        mx = jnp.max(l, axis=-1, keepdims=True)
        idx = jnp.min(jnp.where(l == mx, lane, LANES), axis=-1, keepdims=True)
        vals.append(mx)
        idxs.append(idx)
        l = jnp.where(lane == idx, -jnp.inf, l)
    es = [jnp.exp(v - vals[0]) for v in vals]
    tot = es[0] + es[1] + es[2] + es[3]
    ti = jnp.zeros(logits.shape, jnp.int32)
    gw = jnp.zeros(logits.shape, F32)
    for k in range(TOP_K):
        ti = jnp.where(lane == k, idxs[k], ti)
        gw = jnp.where(lane == k, es[k] / tot, gw)
    topi_o[...] = ti
    gatew_o[...] = gw


def _merge(x2d, o_mla, o_diff, o_mem, lam, mw, tm):
    t = x2d.shape[0]
    row = lambda w: pl.BlockSpec((tm, w), lambda i: (i, 0))
    wnames = ["wg", "wmla", "wdiff", "wmem", "wout", "gattn", "gsub", "gffn", "wr", "br"]
    ws = [mw[k] for k in wnames]
    return pl.pallas_call(
        _merge_kernel,
        grid=(t // tm,),
        in_specs=[pl.BlockSpec(memory_space=pltpu.SMEM), row(D_MODEL), row(o_mla.shape[1]), row(o_diff.shape[1]),
                  row(o_mem.shape[1])] + [_const_spec(w.shape) for w in ws],
        out_specs=[row(D_MODEL), row(D_MODEL), row(LANES), row(LANES)],
        out_shape=[jax.ShapeDtypeStruct((t, D_MODEL), F32), jax.ShapeDtypeStruct((t, D_MODEL), F32),
                   jax.ShapeDtypeStruct((t, LANES), jnp.int32), jax.ShapeDtypeStruct((t, LANES), F32)],
        compiler_params=_cparams(("parallel",)),
        name="merge",
    )(lam, x2d, o_mla, o_diff, o_mem, *ws)


def _dispatch_kernel(pend_ref, dest_ref, h_ref, xs_hbm, zero_sc, sem, *, tm, tmg):
    i = pl.program_id(0)
    n_tiles = xs_hbm.shape[0] // tmg

    @pl.when(i == 0)
    def _():
        zero_sc[...] = jnp.zeros(zero_sc.shape, F32)

        def tail(e):
            end = pend_ref[e + 1]
            return pl.multiple_of(end - tmg, tmg), end > pend_ref[e]

        for e in range(N_EXPERTS):
            start, nonempty = tail(e)

            @pl.when(nonempty)
            def _():
                pltpu.make_async_copy(zero_sc, xs_hbm.at[pl.ds(start, tmg)], sem).start()
        for e in range(N_EXPERTS):
            start, nonempty = tail(e)

            @pl.when(nonempty)
            def _():
                pltpu.make_async_copy(zero_sc, xs_hbm.at[pl.ds(start, tmg)], sem).wait()

        def unused(tile):
            return pltpu.make_async_copy(zero_sc, xs_hbm.at[pl.ds(pl.multiple_of(tile * tmg, tmg), tmg)], sem)

        first_unused = pend_ref[N_EXPERTS] // tmg
        lax.fori_loop(first_unused, n_tiles, lambda tile, c: (unused(tile).start(), c)[1], 0)
        lax.fori_loop(first_unused, n_tiles, lambda tile, c: (unused(tile).wait(), c)[1], 0)

    def issue(r, c):
        for k in range(TOP_K):
            dst = xs_hbm.at[pl.ds(dest_ref[0, 0, r * TOP_K + k], 1)]
            pltpu.make_async_copy(h_ref.at[pl.ds(r, 1)], dst, sem).start(priority=k % 2)
        return c

    lax.fori_loop(0, tm, issue, 0, unroll=ROW_DMA_UNROLL)
    for _ in range(TOP_K):
        pltpu.make_async_copy(h_ref, xs_hbm.at[pl.ds(0, tm)], sem).wait()


def _dispatch(h2, dest, pend0, n_rows, tm, tmg):
    t = h2.shape[0]
    dest3 = dest.reshape(t // tm, 1, tm * TOP_K)
    return pl.pallas_call(
        functools.partial(_dispatch_kernel, tm=tm, tmg=tmg),
        grid_spec=pltpu.PrefetchScalarGridSpec(
            num_scalar_prefetch=1,
            grid=(t // tm,),
            in_specs=[pl.BlockSpec((1, 1, tm * TOP_K), lambda i, p: (i, 0, 0), memory_space=pltpu.SMEM),
                      pl.BlockSpec((tm, D_MODEL), lambda i, p: (i, 0))],
            out_specs=pl.BlockSpec(memory_space=pl.ANY),
            scratch_shapes=[pltpu.VMEM((tmg, D_MODEL), F32), pltpu.SemaphoreType.DMA(())],
        ),
        out_shape=jax.ShapeDtypeStruct((n_rows, D_MODEL), F32),
        compiler_params=_cparams(("arbitrary",)),
        name="moe_dispatch",
    )(pend0, dest3, h2)


def _experts_kernel(te_ref, nu_ref, x_ref, wgu_ref, bgu_ref, wd_ref, bd_ref, o_ref):
    i = pl.program_id(0)

    @pl.when(i < nu_ref[0])
    def _():
        x = x_ref[...].astype(BF)
        gu = jnp.dot(x, wgu_ref[...], preferred_element_type=F32) + bgu_ref[...]
        gate = jnp.minimum(gu[:, :D_EXPERT], SWIGLU_LIMIT)
        up = jnp.clip(gu[:, D_EXPERT:], -SWIGLU_LIMIT, SWIGLU_LIMIT)
        act = (up + 1.0) * (gate * jax.nn.sigmoid(SWIGLU_ALPHA * gate))
        o_ref[...] = jnp.dot(act.astype(BF), wd_ref[...], preferred_element_type=F32) + bd_ref[...]

    @pl.when(i >= nu_ref[0])
    def _():
        o_ref[...] = jnp.zeros(o_ref.shape, F32)


def _experts(xs, tile_expert, n_used, wgu, bgu, wd, bd, tmg):
    n_rows = xs.shape[0]
    n_tiles = n_rows // tmg
    return pl.pallas_call(
        _experts_kernel,
        grid_spec=pltpu.PrefetchScalarGridSpec(
            num_scalar_prefetch=2,
            grid=(n_tiles,),
            in_specs=[pl.BlockSpec((tmg, D_MODEL), lambda i, te, nu: (jnp.minimum(i, nu[0] - 1), 0)),
                      pl.BlockSpec((None, D_MODEL, 2 * D_EXPERT), lambda i, te, nu: (te[i], 0, 0)),
                      pl.BlockSpec((None, 1, 2 * D_EXPERT), lambda i, te, nu: (te[i], 0, 0)),
                      pl.BlockSpec((None, D_EXPERT, D_MODEL), lambda i, te, nu: (te[i], 0, 0)),
                      pl.BlockSpec((None, 1, D_MODEL), lambda i, te, nu: (te[i], 0, 0))],
            out_specs=pl.BlockSpec((tmg, D_MODEL), lambda i, te, nu: (i, 0)),
        ),
        out_shape=jax.ShapeDtypeStruct((n_rows, D_MODEL), F32),
        compiler_params=_cparams(("arbitrary",)),
        name="moe_experts",
    )(tile_expert, n_used, xs, wgu, bgu, wd, bd)


def _combine_kernel(dest_ref, destn_ref, x2_ref, gw_ref, out_hbm, y_ref, buf, sem, *, tm):
    i = pl.program_id(0)
    n = pl.num_programs(0)

    def gather(idx_ref, slot):
        def issue(r, c):
            for k in range(TOP_K):
                src = out_hbm.at[pl.ds(idx_ref[0, 0, r * TOP_K + k], 1)]
                pltpu.make_async_copy(src, buf.at[slot, k, pl.ds(r, 1)], sem.at[slot]).start(priority=k % 2)
            return c

        lax.fori_loop(0, tm, issue, 0, unroll=ROW_DMA_UNROLL)

    slot = i % 2

    @pl.when(i == 0)
    def _():
        gather(dest_ref, 0)

    @pl.when(i + 1 < n)
    def _():
        gather(destn_ref, 1 - slot)

    for k in range(TOP_K):
        pltpu.make_async_copy(out_hbm.at[pl.ds(0, tm)], buf.at[slot, k], sem.at[slot]).wait()
    gw = gw_ref[...]
    y = x2_ref[...]
    for k in range(TOP_K):
        y = y + gw[:, k:k + 1] * buf[slot, k]
    y_ref[...] = y


def _combine(x2, gate_w, dest, out_rows, tm):
    t = x2.shape[0]
    nt = t // tm
    dest3 = dest.reshape(nt, 1, tm * TOP_K)
    idx = lambda f: pl.BlockSpec((1, 1, tm * TOP_K), f, memory_space=pltpu.SMEM)
    return pl.pallas_call(
        functools.partial(_combine_kernel, tm=tm),
        grid=(nt,),
        in_specs=[idx(lambda i: (i, 0, 0)), idx(lambda i: (jnp.minimum(i + 1, nt - 1), 0, 0)),
                  pl.BlockSpec((tm, D_MODEL), lambda i: (i, 0)),
                  pl.BlockSpec((tm, LANES), lambda i: (i, 0)),
                  pl.BlockSpec(memory_space=pl.ANY)],
        out_specs=pl.BlockSpec((tm, D_MODEL), lambda i: (i, 0)),
        out_shape=jax.ShapeDtypeStruct((t, D_MODEL), F32),
        scratch_shapes=[pltpu.VMEM((2, TOP_K, tm, D_MODEL), F32), pltpu.SemaphoreType.DMA((2,))],
        compiler_params=_cparams(("arbitrary",)),
        name="moe_combine",
    )(dest3, dest3, x2, gate_w, out_rows)


def _route(topi, tmg, n_tiles):
    e_ids = jnp.arange(N_EXPERTS, dtype=jnp.int32)
    hit = (topi[:, :, None] == e_ids[None, None, :])
    onehot = jnp.any(hit, axis=1).astype(jnp.int32)
    counts = jnp.sum(onehot, axis=0)
    pos = jnp.cumsum(onehot, axis=0) - onehot
    padded = (counts + tmg - 1) // tmg * tmg
    pend = jnp.cumsum(padded)
    pstart = pend - padded
    base = pstart[None, :] + pos
    dest = jnp.sum(jnp.where(hit, base[:, None, :], 0), axis=-1).astype(jnp.int32)
    n_used = (pend[-1] // tmg).astype(jnp.int32).reshape(1)
    tile_row0 = jnp.arange(n_tiles, dtype=jnp.int32) * tmg
    tile_expert = jnp.minimum(jnp.sum((pend[None, :] <= tile_row0[:, None]).astype(jnp.int32), axis=1),
                              N_EXPERTS - 1).astype(jnp.int32)
    pend0 = jnp.concatenate([jnp.zeros((1,), jnp.int32), pend.astype(jnp.int32)])
    return dest, pend0, n_used, tile_expert


def _moe(x2, h2, topi_pad, gate_w, ew, tm_tok, tmg):
    t = x2.shape[0]
    n_tiles = (t * TOP_K + N_EXPERTS * (tmg - 1) + tmg - 1) // tmg
    dest, pend0, n_used, tile_expert = _route(topi_pad[:, :TOP_K], tmg, n_tiles)
    xs = _dispatch(h2, dest.reshape(-1), pend0, n_tiles * tmg, tm_tok, tmg)
    out_rows = _experts(xs, tile_expert, n_used, ew["wgu"], ew["bgu"], ew["wd"], ew["bd"], tmg)
    return _combine(x2, gate_w, dest.reshape(-1), out_rows, tm_tok)


def _absorb_kernel(qm_ref, wukt_ref, o_ref):
    for hh in range(MLA_HEADS):
        qn = qm_ref[:, hh * LANES:hh * LANES + MLA_NOPE]
        o_ref[hh] = jnp.dot(qn, wukt_ref[hh], preferred_element_type=F32).astype(BF)


def _absorb(qm, wukt):
    ns = qm.shape[0]
    return pl.pallas_call(
        _absorb_kernel,
        in_specs=[_const_spec(qm.shape), _const_spec(wukt.shape)],
        out_specs=_const_spec((MLA_HEADS, ns, KV_RANK)),
        out_shape=jax.ShapeDtypeStruct((MLA_HEADS, ns, KV_RANK), BF),
        grid=(1,),
        compiler_params=_cparams(("arbitrary",)),
        name="sample_q_absorb",
    )(qm, wukt)


def _paged_kernel(pt_ref, qlat_ref, qr_ref, qbd_ref, bias_ref, *rest, pb):
    lat_refs = rest[0:pb]
    krt_refs = rest[pb:2 * pb]
    kt_refs = rest[2 * pb:3 * pb]
    v_refs = rest[3 * pb:4 * pb]
    oa_ref, od_ref, ma, la, acca, md, ld, accd = rest[4 * pb:]
    j = pl.program_id(1)

    @pl.when(j == 0)
    def _():
        ma[...] = jnp.full(ma.shape, -jnp.inf, F32)
        la[...] = jnp.zeros(la.shape, F32)
        acca[...] = jnp.zeros(acca.shape, F32)
        md[...] = jnp.full(md.shape, -jnp.inf, F32)
        ld[...] = jnp.zeros(ld.shape, F32)
        accd[...] = jnp.zeros(accd.shape, F32)

    qlat = qlat_ref[...]
    qr = qr_ref[...]
    qbd = qbd_ref[...]
    rowh = lax.broadcasted_iota(jnp.int32, (2 * DIFF_HEADS, LANES), 0) // 2
    page = lambda a, p: a[:, p * PAGE_SIZE:(p + 1) * PAGE_SIZE]

    s_a = jnp.concatenate(
        [lax.dot_general(qlat, lat_refs[p][...].astype(BF), (((1,), (1,)), ((), ())), preferred_element_type=F32)
         + jnp.dot(qr, krt_refs[p][...].astype(BF), preferred_element_type=F32) for p in range(pb)], axis=1)
    s_d = jnp.concatenate([jnp.dot(qbd, kt_refs[p][...].astype(BF), preferred_element_type=F32) for p in range(pb)],
                          axis=1) + bias_ref[...]

    def softmax_update(s, m_ref, l_ref):
        m_new = jnp.maximum(m_ref[...], jnp.max(s, axis=-1, keepdims=True))
        alpha = jnp.exp(m_ref[...] - m_new)
        pe = jnp.exp(s - m_new)
        l_ref[...] = alpha * l_ref[...] + jnp.sum(pe, axis=-1, keepdims=True)
        m_ref[...] = m_new
        return alpha, pe.astype(BF)

    alpha_a, pe_a = softmax_update(s_a, ma, la)
    alpha_d, pe_d = softmax_update(s_d, md, ld)

    pv = jnp.dot(page(pe_a, 0), lat_refs[0][...].astype(BF), preferred_element_type=F32)
    for p in range(1, pb):
        pv = pv + jnp.dot(page(pe_a, p), lat_refs[p][...].astype(BF), preferred_element_type=F32)
    acca[...] = alpha_a * acca[...] + pv

    pv = jnp.zeros((2 * DIFF_HEADS, DIFF_VD), F32)
    for hh in range(DIFF_HEADS):
        pvh = None
        for p in range(pb):
            vh = v_refs[p][pl.ds(hh, PAGE_SIZE, stride=DIFF_HEADS), :].astype(BF)
            part = jnp.dot(page(pe_d, p), vh, preferred_element_type=F32)
            pvh = part if pvh is None else pvh + part
        pv = jnp.where(rowh == hh, pvh, pv)
    accd[...] = alpha_d * accd[...] + pv

    @pl.when(j == pl.num_programs(1) - 1)
    def _():
        oa_ref[:, 0:KV_RANK] = acca[...]
        oa_ref[:, KV_RANK:KV_RANK + LANES] = jnp.broadcast_to(ma[...], (MLA_HEADS, LANES))
        oa_ref[:, KV_RANK + LANES:] = jnp.broadcast_to(la[...], (MLA_HEADS, LANES))
        od_ref[:, 0:DIFF_VD] = accd[...]
        od_ref[:, DIFF_VD:2 * DIFF_VD] = jnp.broadcast_to(md[...], (2 * DIFF_HEADS, LANES))
        od_ref[:, 2 * DIFF_VD:] = jnp.broadcast_to(ld[...], (2 * DIFF_HEADS, LANES))


def _paged_attention(page_table, qlat, qr, qbd, bias, lat_pages, krt_pages, kt_pages, v_pages, pb):
    ns, n_pages = page_table.shape
    assert n_pages % pb == 0

    def page_spec(shape, p):
        return pl.BlockSpec((None,) + shape, lambda n, j, pt: (pt[n, j * pb + p], 0, 0))

    per_n = lambda shape: pl.BlockSpec((None,) + shape, lambda n, j, pt: (n, 0, 0))
    in_specs = [per_n((MLA_HEADS, KV_RANK)), per_n((MLA_HEADS, MLA_ROPE)), per_n((2 * DIFF_HEADS, 512)),
                pl.BlockSpec((2 * DIFF_HEADS, pb * PAGE_SIZE), lambda n, j, pt: (0, j))]
    ins = [qlat, qr, qbd, bias]
    for arr, shape in ((lat_pages, (PAGE_SIZE, KV_RANK)), (krt_pages, (MLA_ROPE, PAGE_SIZE)),
                       (kt_pages, (512, PAGE_SIZE)), (v_pages, (512, DIFF_VD))):
        for p in range(pb):
            in_specs.append(page_spec(shape, p))
            ins.append(arr)
    wa = KV_RANK + 2 * LANES
    wd = 3 * DIFF_VD
    return pl.pallas_call(
        functools.partial(_paged_kernel, pb=pb),
        grid_spec=pltpu.PrefetchScalarGridSpec(
            num_scalar_prefetch=1,
            grid=(ns, n_pages // pb),
            in_specs=in_specs,
            out_specs=[per_n((MLA_HEADS, wa)), per_n((2 * DIFF_HEADS, wd))],
            scratch_shapes=[pltpu.VMEM((MLA_HEADS, 1), F32), pltpu.VMEM((MLA_HEADS, 1), F32),
                            pltpu.VMEM((MLA_HEADS, KV_RANK), F32),
                            pltpu.VMEM((2 * DIFF_HEADS, 1), F32), pltpu.VMEM((2 * DIFF_HEADS, 1), F32),
                            pltpu.VMEM((2 * DIFF_HEADS, DIFF_VD), F32)],
        ),
        out_shape=[jax.ShapeDtypeStruct((ns, MLA_HEADS, wa), F32),
                   jax.ShapeDtypeStruct((ns, 2 * DIFF_HEADS, wd), F32)],
        compiler_params=_cparams(("parallel", "arbitrary")),
        name="sample_paged_attn",
    )(page_table, *ins)


def _finish_kernel(b0_ref, pa_ref, pd_ref, qlat_ref, qm_ref, qd_ref, ckv_ref, kr_ref, kd_ref, vd_ref, wuv_ref,
                   omla_o, odiff_o):
    ckv = ckv_ref[...]
    kr = kr_ref[...]
    krs = pltpu.roll(kr, MLA_NOPE, 1)
    lane = lax.broadcasted_iota(jnp.int32, (1, LANES), 1)
    rope_lanes = (lane >= MLA_NOPE) & (lane < MLA_QK)
    for hh in range(MLA_HEADS):
        acc = pa_ref[hh, :, 0:KV_RANK]
        m = pa_ref[hh, :, KV_RANK:KV_RANK + 1]
        l = pa_ref[hh, :, KV_RANK + LANES:KV_RANK + LANES + 1]
        qg = qm_ref[:, hh * LANES:(hh + 1) * LANES].astype(F32)
        s = (jnp.sum(qlat_ref[hh].astype(F32) * ckv, axis=-1, keepdims=True)
             + jnp.sum(jnp.where(rope_lanes, qg * krs, 0.0), axis=-1, keepdims=True))
        m_new = jnp.maximum(m, s)
        alpha = jnp.exp(m - m_new)
        pn = jnp.exp(s - m_new)
        l = alpha * l + pn
        acc = alpha * acc + pn * ckv
        o_lat = (acc / l).astype(BF)
        omla_o[:, hh * LANES:(hh + 1) * LANES] = jnp.dot(
            o_lat, wuv_ref[:, hh * LANES:(hh + 1) * LANES], preferred_element_type=F32).astype(BF)
    for g in range(2 * DIFF_HEADS):
        hh = g // 2
        acc = pd_ref[g, :, 0:DIFF_VD]
        m = pd_ref[g, :, DIFF_VD:DIFF_VD + 1]
        l = pd_ref[g, :, 2 * DIFF_VD:2 * DIFF_VD + 1]
        qg = qd_ref[:, g * LANES:(g + 1) * LANES].astype(F32)
        s = jnp.sum(qg * kd_ref[:, hh * LANES:(hh + 1) * LANES], axis=-1, keepdims=True) + b0_ref[hh]
        m_new = jnp.maximum(m, s)
        alpha = jnp.exp(m - m_new)
        pn = jnp.exp(s - m_new)
        l = alpha * l + pn
        acc = alpha * acc + pn * vd_ref[:, hh * LANES:(hh + 1) * LANES]
        odiff_o[:, g * LANES:(g + 1) * LANES] = acc / l


def _finish(b0, pa, pd, qlat, qm, qd, ckv, kr128, kd, vd, wuv):
    ns = qm.shape[0]
    ins = [pa, pd, qlat, qm, qd, ckv, kr128, kd, vd, wuv]
    return pl.pallas_call(
        _finish_kernel,
        grid=(1,),
        in_specs=[pl.BlockSpec(memory_space=pltpu.SMEM)] + [_const_spec(a.shape) for a in ins],
        out_specs=[_const_spec((ns, MLA_HEADS * LANES)), _const_spec((ns, 2 * DIFF_HEADS * LANES))],
        out_shape=[jax.ShapeDtypeStruct((ns, MLA_HEADS * LANES), BF),
                   jax.ShapeDtypeStruct((ns, 2 * DIFF_HEADS * LANES), F32)],
        compiler_params=_cparams(("arbitrary",)),
        name="sample_attn_finish",
    )(b0, *ins)


def _mem_decode_kernel(q_ref, k_ref, v_ref, o_ref, *, m_len):
    q = q_ref[...]
    rows = lax.broadcasted_iota(jnp.int32, (8, LANES), 0)
    out = jnp.zeros((8, LANES), F32)
    for hh in range(MEM_HEADS):
        kh = k_ref[pl.ds(hh, m_len, stride=MEM_HEADS), :].astype(BF)
        vh = v_ref[pl.ds(hh, m_len, stride=MEM_HEADS), :].astype(BF)
        s = lax.dot_general(q, kh, (((1,), (1,)), ((), ())), preferred_element_type=F32)
        m = jnp.max(s, axis=-1, keepdims=True)
        pe = jnp.exp(s - m)
        l = jnp.sum(pe, axis=-1, keepdims=True)
        o = jnp.dot(pe.astype(BF), vh, preferred_element_type=F32) / l
        out = jnp.where(rows == hh, o, out)
    o_ref[...] = out.astype(o_ref.dtype)


def _mem_decode(q8, mem_k, mem_v, m_len):
    ns = q8.shape[0]
    blk = lambda r: pl.BlockSpec((None, r, LANES), lambda n: (n, 0, 0))
    return pl.pallas_call(
        functools.partial(_mem_decode_kernel, m_len=m_len),
        grid=(ns,),
        in_specs=[blk(8), blk(m_len * MEM_HEADS), blk(m_len * MEM_HEADS)],
        out_specs=blk(8),
        out_shape=jax.ShapeDtypeStruct((ns, 8, LANES), BF),
        compiler_params=_cparams(("parallel",)),
        name="sample_mem_attn",
    )(q8, mem_k, mem_v)


def _sample_path(x_sample, caches, page_table, t5_bias, pw, lam, w_uk, w_uv):
    cache_lat, cache_kr, cache_dk, cache_dv, cache_mk, cache_mv = caches
    ns, t1, d = x_sample.shape
    assert t1 == 1, "one new token per sample"
    n_pages = page_table.shape[1]
    pos = jnp.full((ns,), n_pages * PAGE_SIZE, jnp.int32)
    (qm, ckv, kr, qd, kd, kdb, vd, vdb, qmem) = _project(
        x_sample.reshape(ns, d), _rope_lane_tables(pos), pw["wp"], pw["gains"], ns, False)
    n_pool = cache_lat.shape[1]
    lat_pages = cache_lat[0]
    krt_pages = jnp.transpose(cache_kr[0], (0, 2, 1))
    kt_pages = jnp.transpose(cache_dk[0], (0, 2, 3, 4, 1)).reshape(n_pool, 512, PAGE_SIZE)
    v_pages = cache_dv[0].reshape(n_pool, PAGE_SIZE * DIFF_HEADS, DIFF_VD)
    m_len = cache_mk.shape[2]
    mem_k = cache_mk[0].reshape(ns, m_len * MEM_HEADS, MEM_HD)
    mem_v = cache_mv[0].reshape(ns, m_len * MEM_HEADS, MEM_HD)

    wukt = jnp.transpose(w_uk, (1, 2, 0)).astype(BF)
    qlat_h = _absorb(qm, wukt)
    qlat = jnp.transpose(qlat_h, (1, 0, 2))
    qr = qm.reshape(ns, MLA_HEADS, LANES)[:, :, MLA_NOPE:MLA_QK]
    qd4 = qd.reshape(ns, DIFF_HEADS, 2, LANES)
    z = jnp.zeros_like(qd4)
    qbd = jnp.stack([jnp.where(jnp.arange(DIFF_HEADS)[None, :, None, None] == hh, qd4, z)
                     for hh in range(DIFF_HEADS)], axis=3)
    qbd = qbd.reshape(ns, 2 * DIFF_HEADS, DIFF_HEADS * LANES)
    q_pos = n_pages * PAGE_SIZE
    bias = _t5_bucket_bias(t5_bias, q_pos - jnp.arange(n_pages * PAGE_SIZE))
    bias = jnp.repeat(bias, 2, axis=0)
    b0 = _t5_bucket_bias(t5_bias, jnp.zeros((1,), jnp.int32))[:, 0]
    pa, pd = _paged_attention(page_table, qlat, qr, qbd, bias, lat_pages, krt_pages, kt_pages, v_pages, pw["pb"])
    kr128 = jnp.pad(kr, ((0, 0), (0, LANES - MLA_ROPE)))
    o_mla, o_diff = _finish(b0, jnp.transpose(pa, (1, 0, 2)), jnp.transpose(pd, (1, 0, 2)), qlat_h, qm, qd, ckv,
                            kr128, kd, vd.reshape(ns, DIFF_HEADS * DIFF_VD), pw["wuv"])
    q8 = jnp.pad(qmem.reshape(ns, MEM_HEADS, MEM_HD), ((0, 0), (0, 8 - MEM_HEADS), (0, 0)))
    o_mem = _mem_decode(q8, mem_k, mem_v, m_len)[:, :MEM_HEADS].reshape(ns, MEM_HEADS * MEM_HD)
    x2, h2, topi, gw = _merge(x_sample.reshape(ns, d), o_mla, o_diff, o_mem, lam, pw["merge"], ns)
    y = _moe(x2, h2, topi, gw, pw["experts"], ns, pw["tmg_sample"])
    return (y.reshape(ns, 1, d), ckv.reshape(1, ns, 1, KV_RANK), kr.reshape(1, ns, 1, MLA_ROPE),
            kd.reshape(1, ns, 1, DIFF_HEADS, 2, DIFF_HD), vd.reshape(1, ns, 1, DIFF_HEADS, DIFF_VD))


def _prompt_path(x_prompt, mem_prompt, t5_bias, pw, lam):
    n, s, d = x_prompt.shape
    t = n * s
    pos_tab = _rope_lane_tables(jnp.arange(s))
    (qm, ckv, kr, qd, kd, kdb, vd, vdb, qmem, kmla, vmla) = _project(
        x_prompt.reshape(t, d), pos_tab, pw["wp"], pw["gains"], pw["tm_proj"], True, pw["wuk"], pw["wuv"])
    tq = pw["tq"]
    r3 = lambda a: a.reshape(n, s, a.shape[-1])
    o_mla = _flash(r3(qm), r3(kmla), r3(vmla), None, causal=True, kv_shared=False, hps=2, tq=tq, tk=tq,
                   out_dtype=BF, name="attn_mla")
    o_diff = _flash(r3(qd), r3(kdb), r3(vdb), _diff_bias_tiles(t5_bias, tq), causal=True, kv_shared=True, hps=2,
                    tq=tq, tk=tq, out_dtype=F32, name="attn_diff")
    m_len = mem_prompt.shape[1]
    mk, mv, mkb, mvb = _memory_kv(mem_prompt.reshape(n * m_len, d), pw["w_mem_kv"], pw["g_mem"], pw["g_k_mem"],
                                  min(512, n * m_len))
    rm = lambda a: a.reshape(n, m_len, a.shape[-1])
    o_mem = _flash(r3(qmem), rm(mkb), rm(mvb), None, causal=False, kv_shared=False, hps=2, tq=tq, tk=m_len,
                   out_dtype=BF, name="attn_mem")
    x2, h2, topi, gw = _merge(x_prompt.reshape(t, d), o_mla.reshape(t, -1), o_diff.reshape(t, -1),
                              o_mem.reshape(t, -1), lam, pw["merge"], pw["tm_merge"])
    y = _moe(x2, h2, topi, gw, pw["experts"], pw["tm_tok"], pw["tmg"])
    outs = (y.reshape(n, s, d), ckv.reshape(1, n, s, KV_RANK), jnp.transpose(kr, (0, 2, 1))[None],
            jnp.transpose(kd.reshape(1, n, DIFF_HEADS, 2, DIFF_HD, s), (0, 1, 5, 2, 3, 4)),
            vd.reshape(1, n, s, DIFF_HEADS, DIFF_VD),
            mk.reshape(1, n, m_len, MEM_HEADS, MEM_HD), mv.reshape(1, n, m_len, MEM_HEADS, MEM_HD))
    return outs


def _prepare(g_attn, w_in, g_q_mla, g_ckv, g_krope, w_uk, w_uv, g_q_diff, g_k_diff, g_subln, g_mem, w_mem_kv,
             g_q_mem, g_k_mem, w_br_mla, w_br_diff, w_br_mem, w_out, g_ffn, w_router, b_router, w_gate_up,
             b_gate_up, w_down, b_down):
    wp, wg = _pack_proj_weights(w_in, g_q_mla, g_krope)
    gains = [g_attn.reshape(1, -1),
             _pad_lanes(g_q_mla * (MLA_QK ** -0.5)),
             g_ckv.reshape(1, -1),
             _pad_lanes(g_krope),
             jnp.tile(g_q_diff * (DIFF_HD ** -0.5), 2).reshape(1, -1),
             jnp.tile(g_k_diff, 2).reshape(1, -1),
             (g_q_mem * (MEM_HD ** -0.5)).reshape(1, -1)]
    wuk = jnp.pad(w_uk, ((0, 0), (0, 0), (0, LANES - MLA_NOPE))).reshape(KV_RANK, MLA_HEADS * LANES).astype(BF)
    wuv = jnp.pad(w_uv, ((0, 0), (0, 0), (0, LANES - MLA_VD))).reshape(KV_RANK, MLA_HEADS * LANES).astype(BF)
    wmla = jnp.pad(w_br_mla.reshape(MLA_HEADS, MLA_VD, D_MODEL), ((0, 0), (0, LANES - MLA_VD), (0, 0)))
    wmla = wmla.reshape(MLA_HEADS * LANES, D_MODEL).astype(BF)
    wr = jnp.pad(w_router, ((0, 0), (0, LANES - N_EXPERTS)))
    br = jnp.concatenate([b_router, jnp.full((LANES - N_EXPERTS,), NEG_BIG, F32)]).reshape(1, LANES)
    merge = dict(wg=wg, wmla=wmla, wdiff=w_br_diff.astype(BF), wmem=w_br_mem.astype(BF), wout=w_out.astype(BF),
                 gattn=g_attn.reshape(1, -1), gsub=g_subln.reshape(1, -1), gffn=g_ffn.reshape(1, -1), wr=wr, br=br)
    experts = dict(wgu=w_gate_up.astype(BF), bgu=b_gate_up.reshape(N_EXPERTS, 1, -1), wd=w_down.astype(BF),
                   bd=b_down.reshape(N_EXPERTS, 1, -1))
    return dict(wp=wp, gains=gains, wuk=wuk, wuv=wuv, merge=merge, experts=experts, w_mem_kv=w_mem_kv,
                g_mem=g_mem, g_k_mem=g_k_mem)


def kernel(x_prompt, x_sample, mem_prompt, cache_mla_latent, cache_mla_krope, cache_diff_k, cache_diff_v,
           cache_mem_k, cache_mem_v, page_table, t5_bias, g_attn, w_in, g_q_mla, g_ckv, g_krope, w_uk, w_uv,
           g_q_diff, g_k_diff, lambda_q1, lambda_k1, lambda_q2, lambda_k2, g_subln, g_mem, w_mem_kv, g_q_mem,
           g_k_mem, w_br_mla, w_br_diff, w_br_mem, w_out, g_ffn, w_router, b_router, w_gate_up, b_gate_up,
           w_down, b_down):
    assert g_attn.shape[0] == 1, "single-layer trunk"
    l = 0
    pw = _prepare(g_attn[l], w_in[l], g_q_mla[l], g_ckv[l], g_krope[l], w_uk[l], w_uv[l], g_q_diff[l], g_k_diff[l],
                  g_subln[l], g_mem[l], w_mem_kv[l], g_q_mem[l], g_k_mem[l], w_br_mla[l], w_br_diff[l],
                  w_br_mem[l], w_out[l], g_ffn[l], w_router[l], b_router[l], w_gate_up[l], b_gate_up[l],
                  w_down[l], b_down[l])
    lam = (jnp.exp(jnp.sum(lambda_q1[l] * lambda_k1[l]).astype(F32))
           - jnp.exp(jnp.sum(lambda_q2[l] * lambda_k2[l]).astype(F32)) + LAM_INIT).reshape(1)
    s = x_prompt.shape[1]
    pw.update(tm_proj=min(TILE_PROJ, s), tq=min(TILE_ATTN, s), tm_merge=min(TILE_MERGE, s),
              tm_tok=min(TILE_TOKEN, s), tmg=TILE_EXPERT, tmg_sample=TILE_EXPERT_SAMPLE, pb=PAGES_PER_STEP)
    p = _prompt_path(x_prompt, mem_prompt, t5_bias, pw, lam)
    caches = (cache_mla_latent, cache_mla_krope, cache_diff_k, cache_diff_v, cache_mem_k, cache_mem_v)
    sm = _sample_path(x_sample, caches, page_table, t5_bias, pw, lam, w_uk[l], w_uv[l])
    return (p[0], sm[0]) + p[1:] + sm[1:]
```

```python
import functools
import math

import jax
import jax.numpy as jnp
import numpy as np
from jax import lax
from jax.experimental import pallas as pl
from jax.experimental.pallas import tpu as pltpu

D_MODEL = 1024
PAGE_SIZE = 128
MLA_HEADS = 8
MLA_NOPE = 64
MLA_ROPE = 32
MLA_QK = MLA_NOPE + MLA_ROPE
MLA_VD = 64
KV_RANK = 256
ROPE_THETA = 10000.0
DIFF_HEADS = 4
DIFF_HD = 64
DIFF_VD = 2 * DIFF_HD
MEM_HEADS = 4
MEM_HD = 128
N_BUCKETS = 32
MAX_EXACT = N_BUCKETS // 2
MAX_DISTANCE = 128
N_EXPERTS = 32
TOP_K = 4
D_EXPERT = D_MODEL
SWIGLU_LIMIT = 7.0
SWIGLU_ALPHA = 1.702
EPS = 1e-6
N_BRANCHES = 3
LAM_INIT = 0.8 - 0.6 * math.exp(-0.3 * 0)

LANES = 128
SUBLANES = 8
VMEM_LIMIT = 56 * 1024 * 1024
NEG_BIG = -1e30
TILE_PROJ = 512
TILE_ATTN = 512
TILE_MERGE = 512
TILE_TOKEN = 512

TILE_EXPERT = 512
TILE_EXPERT_SAMPLE = 128
PAGES_PER_STEP = 16

C_QM = 0
C_CKV = C_QM + MLA_HEADS * LANES
C_KR = C_CKV + KV_RANK
C_QD = C_KR + LANES
C_KD = C_QD + 512
C_VD = C_KD + 512
C_QMEM = C_VD + 512
C_END = C_QMEM + 512

BF = jnp.bfloat16
F32 = jnp.float32


def _cparams(sem):
    return pltpu.CompilerParams(dimension_semantics=sem, vmem_limit_bytes=VMEM_LIMIT)


def _const_spec(shape):
    nd = len(shape)
    return pl.BlockSpec(shape, lambda *a: (0,) * nd)


def _rsqrt_mean(sumsq, n):
    return lax.rsqrt(sumsq * (1.0 / n) + EPS)


def _proj_kernel(x_ref, w_ref, gattn_ref, gq_ref, gckv_ref, gkr_ref, gqd_ref, gkd_ref, gqm_ref,
                 cq_ref, sq_ref, ck_ref, sk_ref, *rest, prompt):
    if prompt:
        (wuk_ref, wuv_ref, qm_o, ckv_o, kr_o, qd_o, kd_o, kdb_o, vd_o, vdb_o, qmem_o, kmla_o, vmla_o) = rest
    else:
        (qm_o, ckv_o, kr_o, qd_o, kd_o, kdb_o, vd_o, vdb_o, qmem_o) = rest
    x = x_ref[...]
    r = _rsqrt_mean(jnp.sum(x * x, axis=-1, keepdims=True), D_MODEL)
    h = (x * r * gattn_ref[...]).astype(BF)
    lane = lax.broadcasted_iota(jnp.int32, (1, LANES), 1)
    lo64 = lane < 64

    def seg(c0, n):
        return jnp.dot(h, w_ref[:, c0:c0 + n], preferred_element_type=F32)

    zq = seg(C_QM, MLA_HEADS * LANES)
    cq, sq = cq_ref[...], sq_ref[...]
    gq = gq_ref[...]
    for g in range(MLA_HEADS):
        z = zq[:, g * LANES:(g + 1) * LANES]
        zm = jnp.where(lane < MLA_QK, z, 0.0)
        rr = _rsqrt_mean(jnp.sum(zm * zm, axis=-1, keepdims=True), MLA_QK)
        y = zm * rr * gq
        yrot = pltpu.roll(z, LANES - MLA_ROPE, 1) * rr
        qm_o[:, g * LANES:(g + 1) * LANES] = (y * cq + yrot * sq).astype(BF)

    zc = seg(C_CKV, KV_RANK)
    rr = _rsqrt_mean(jnp.sum(zc * zc, axis=-1, keepdims=True), KV_RANK)
    ckv = zc * rr * gckv_ref[...]
    ckv_o[...] = ckv

    zk = seg(C_KR, LANES)
    zm = jnp.where(lane < MLA_ROPE, zk, 0.0)
    rr = _rsqrt_mean(jnp.sum(zm * zm, axis=-1, keepdims=True), MLA_ROPE)
    kr = zm * rr * gkr_ref[...] * ck_ref[...] + pltpu.roll(zk, LANES - MLA_ROPE, 1) * rr * sk_ref[...]
    if prompt:
        kr_o[...] = kr.T[:MLA_ROPE, :]
    else:
        kr_o[...] = kr[:, :MLA_ROPE]

    def halfnorm(z, gain):
        sq_ = z * z
        s_lo = jnp.sum(jnp.where(lo64, sq_, 0.0), axis=-1, keepdims=True)
        s_hi = jnp.sum(jnp.where(lo64, 0.0, sq_), axis=-1, keepdims=True)
        rr_ = jnp.where(lo64, _rsqrt_mean(s_lo, DIFF_HD), _rsqrt_mean(s_hi, DIFF_HD))
        return z * rr_ * gain

    zqd = seg(C_QD, 512)
    gqd = gqd_ref[...]
    for hh in range(DIFF_HEADS):
        y = halfnorm(zqd[:, hh * LANES:(hh + 1) * LANES], gqd)
        qd_o[:, (2 * hh) * LANES:(2 * hh + 1) * LANES] = jnp.where(lo64, y, 0.0).astype(BF)
        qd_o[:, (2 * hh + 1) * LANES:(2 * hh + 2) * LANES] = jnp.where(lo64, 0.0, y).astype(BF)

    zkd = seg(C_KD, 512)
    gkd = gkd_ref[...]
    for hh in range(DIFF_HEADS):
        y = halfnorm(zkd[:, hh * LANES:(hh + 1) * LANES], gkd)
        if prompt:
            kd_o[hh * LANES:(hh + 1) * LANES, :] = y.T
        else:
            kd_o[:, hh * LANES:(hh + 1) * LANES] = y
        kdb_o[:, hh * LANES:(hh + 1) * LANES] = y.astype(BF)

    zv = seg(C_VD, 512)
    tm = zv.shape[0]
    for hh in range(DIFF_HEADS):
        vd_o[pl.ds(hh, tm, stride=DIFF_HEADS), :] = zv[:, hh * LANES:(hh + 1) * LANES]
    vdb_o[...] = zv.astype(BF)

    zm_ = seg(C_QMEM, 512)
    gqm = gqm_ref[...]
    for hh in range(MEM_HEADS):
        z = zm_[:, hh * LANES:(hh + 1) * LANES]
        rr = _rsqrt_mean(jnp.sum(z * z, axis=-1, keepdims=True), MEM_HD)
        qmem_o[:, hh * LANES:(hh + 1) * LANES] = (z * rr * gqm).astype(BF)

    if prompt:
        cb = ckv.astype(BF)
        kn = jnp.dot(cb, wuk_ref[...], preferred_element_type=F32)
        krs = pltpu.roll(kr, MLA_NOPE, 1)
        for g in range(MLA_HEADS):
            kmla_o[:, g * LANES:(g + 1) * LANES] = (kn[:, g * LANES:(g + 1) * LANES] + krs).astype(BF)
        vmla_o[...] = jnp.dot(cb, wuv_ref[...], preferred_element_type=F32).astype(BF)


def _rope_lane_tables(pos):
    inv = ROPE_THETA ** (-jnp.arange(0, MLA_ROPE, 2, dtype=F32) / MLA_ROPE)
    ang = pos.astype(F32)[:, None] * inv[None, :]
    cos, sin = jnp.cos(ang), jnp.sin(ang)
    n = pos.shape[0]
    one = jnp.ones((n, MLA_NOPE), F32)
    z32 = jnp.zeros((n, 32), F32)
    z64 = jnp.zeros((n, 64), F32)
    cq = jnp.concatenate([one, cos, cos, z32], axis=1)
    sq = jnp.concatenate([z64, sin, sin, z32], axis=1)
    ck = jnp.concatenate([cos, cos, z32, z64], axis=1)
    sk = jnp.concatenate([sin, sin, z32, z64], axis=1)
    return cq, sq, ck, sk


def _rot_cols(w, g):
    half = MLA_ROPE // 2
    return jnp.concatenate([-w[..., half:] * g[half:], w[..., :half] * g[:half]], axis=-1)


def _pack_proj_weights(w_in, g_q_mla, g_krope):
    d = w_in.shape[0]
    cuts = np.cumsum([MLA_HEADS * MLA_QK, KV_RANK, MLA_ROPE, 512, 512, 512, 512])
    wq = w_in[:, :cuts[0]].reshape(d, MLA_HEADS, MLA_QK)
    scale = MLA_QK ** -0.5
    rotq = _rot_cols(wq[..., MLA_NOPE:], g_q_mla[MLA_NOPE:] * scale)
    wq = jnp.concatenate([wq, rotq], axis=-1).reshape(d, MLA_HEADS * LANES)
    wc = w_in[:, cuts[0]:cuts[1]]
    wk = w_in[:, cuts[1]:cuts[2]]
    wk = jnp.concatenate([wk, _rot_cols(wk, g_krope), jnp.zeros((d, 64), F32)], axis=-1)
    rest = w_in[:, cuts[2]:cuts[6]]
    wp = jnp.concatenate([wq, wc, wk, rest], axis=1).astype(BF)
    wg = w_in[:, cuts[6]:].astype(BF)
    return wp, wg


def _pad_lanes(v, n=LANES):
    return jnp.pad(v, (0, n - v.shape[0])).reshape(1, n)


def _project(x2d, pos_tab, wp, gains, tm, prompt, wuk=None, wuv=None):
    t = x2d.shape[0]
    cq, sq, ck, sk = pos_tab
    nper = cq.shape[0] // tm
    row = lambda w: pl.BlockSpec((tm, w), lambda i: (i, 0))
    tab = pl.BlockSpec((tm, LANES), lambda i: (i % nper, 0))
    ins = [x2d, wp] + list(gains) + [cq, sq, ck, sk]
    in_specs = ([row(D_MODEL), _const_spec(wp.shape)] + [_const_spec(g.shape) for g in gains] + [tab] * 4)
    rows = lambda w, dt: (jax.ShapeDtypeStruct((t, w), dt), row(w))
    if prompt:
        nb = t // cq.shape[0]
        tr = lambda w: (jax.ShapeDtypeStruct((nb, w, cq.shape[0]), F32),
                        pl.BlockSpec((None, w, tm), lambda i: (i // nper, 0, i % nper)))
        kr_out, kd_out = tr(MLA_ROPE), tr(512)
    else:
        kr_out, kd_out = rows(MLA_ROPE, F32), rows(512, F32)
    vd_out = (jax.ShapeDtypeStruct((t * DIFF_HEADS, DIFF_VD), F32),
              pl.BlockSpec((tm * DIFF_HEADS, DIFF_VD), lambda i: (i, 0)))
    outs = [rows(1024, BF), rows(KV_RANK, F32), kr_out, rows(1024, BF), kd_out, rows(512, BF), vd_out,
            rows(512, BF), rows(512, BF)]
    if prompt:
        ins += [wuk, wuv]
        in_specs += [_const_spec(wuk.shape), _const_spec(wuv.shape)]
        outs += [rows(1024, BF), rows(1024, BF)]
    return pl.pallas_call(
        functools.partial(_proj_kernel, prompt=prompt),
        grid=(t // tm,),
        in_specs=in_specs,
        out_specs=[spec for _, spec in outs],
        out_shape=[shape for shape, _ in outs],
        compiler_params=_cparams(("parallel",)),
        name="proj_prompt" if prompt else "proj_sample",
    )(*ins)


def _flash_kernel(q_ref, k_ref, v_ref, *rest, causal, has_bias, tq, tk, nk, hps, kv_shared, pack64):
    if has_bias:
        bias_ref, o_ref, m_sc, l_sc, acc_sc = rest
    else:
        o_ref, m_sc, l_sc, acc_sc = rest
    i = pl.program_id(2)
    m_sc[...] = jnp.full(m_sc.shape, -jnp.inf, F32)
    l_sc[...] = jnp.zeros(l_sc.shape, F32)
    acc_sc[...] = jnp.zeros(acc_sc.shape, F32)

    def step(j, masked):
        off = pl.multiple_of(j * tk, tk)
        scores = []
        for c in range(hps):
            kc = 0 if kv_shared else c
            q = q_ref[:, c * LANES:(c + 1) * LANES]
            k = k_ref[pl.ds(off, tk), kc * LANES:(kc + 1) * LANES]
            s = lax.dot_general(q, k, (((1,), (1,)), ((), ())), preferred_element_type=F32)
            if has_bias:
                s = s + bias_ref[jnp.minimum(i - j, 2)]
            elif masked:
                rows = lax.broadcasted_iota(jnp.int32, (tq, tk), 0)
                cols = lax.broadcasted_iota(jnp.int32, (tq, tk), 1)
                s = jnp.where(rows >= cols, s, NEG_BIG)
            scores.append(s)
        for c in range(hps):
            kc = 0 if kv_shared else c
            s = scores[c]
            v = v_ref[pl.ds(off, tk), kc * LANES:(kc + 1) * LANES]
            m_prev = m_sc[c]
            m_new = jnp.maximum(m_prev, jnp.max(s, axis=-1, keepdims=True))
            alpha = jnp.exp(m_prev - m_new)
            p = jnp.exp(s - jnp.concatenate([m_new] * (tk // LANES), axis=1))
            l_sc[c] = alpha * l_sc[c] + jnp.sum(p, axis=-1, keepdims=True)
            acc_sc[c] = alpha * acc_sc[c] + jnp.dot(p.astype(BF), v, preferred_element_type=F32)
            m_sc[c] = m_new

    if causal:
        def body(j, c):
            step(j, False)
            return c
        lax.fori_loop(0, i, body, 0)
        step(i, True)
    else:
        for j in range(nk):
            step(j, False)
    if pack64:
        lane = lax.broadcasted_iota(jnp.int32, (1, LANES), 1)
        o1 = pltpu.roll(acc_sc[1] / l_sc[1], LANES // 2, 1)
        o_ref[...] = jnp.where(lane < LANES // 2, acc_sc[0] / l_sc[0], o1).astype(o_ref.dtype)
    else:
        for c in range(hps):
            o_ref[:, c * LANES:(c + 1) * LANES] = (acc_sc[c] / l_sc[c]).astype(o_ref.dtype)


def _flash(q, k, v, bias, *, causal, kv_shared, hps, tq, tk, out_dtype, name, pack64=False):
    n, sq_len, hq = q.shape[0], q.shape[1], q.shape[2] // LANES
    sk_len = k.shape[1]
    has_bias = bias is not None
    kvw = LANES if kv_shared else hps * LANES
    in_specs = [pl.BlockSpec((None, tq, hps * LANES), lambda b, h, i: (b, i, h)),
                pl.BlockSpec((None, sk_len, kvw), lambda b, h, i: (b, 0, h)),
                pl.BlockSpec((None, sk_len, kvw), lambda b, h, i: (b, 0, h))]
    ins = [q, k, v]
    if has_bias:
        assert kv_shared and bias.shape[0] == hq // hps
        in_specs.append(pl.BlockSpec((None, 3, tq, tk), lambda b, h, i: (h, 0, 0, 0)))
        ins.append(bias)
    assert not pack64 or hps == 2
    ow = LANES if pack64 else hps * LANES
    return pl.pallas_call(
        functools.partial(_flash_kernel, causal=causal, has_bias=has_bias, tq=tq, tk=tk, nk=sk_len // tk, hps=hps,
                          kv_shared=kv_shared, pack64=pack64),
        grid=(n, hq // hps, sq_len // tq),
        in_specs=in_specs,
        out_specs=pl.BlockSpec((None, tq, ow), lambda b, h, i: (b, i, h)),
        out_shape=jax.ShapeDtypeStruct((n, sq_len, (hq // hps) * ow), out_dtype),
        scratch_shapes=[pltpu.VMEM((hps, tq, LANES), F32)] * 3,
        compiler_params=_cparams(("parallel", "parallel", "arbitrary")),
        name=name,
    )(*ins)


def _t5_bucket_bias(table, dist):
    n = jnp.maximum(dist, 0)
    nf = jnp.maximum(n, 1).astype(F32)
    large = MAX_EXACT + (jnp.log(nf / MAX_EXACT) / math.log(MAX_DISTANCE / MAX_EXACT)
                         * (N_BUCKETS - MAX_EXACT)).astype(jnp.int32)
    bucket = jnp.where(n < MAX_EXACT, n, jnp.minimum(large, N_BUCKETS - 1))
    tab = table.astype(F32)
    out = jnp.zeros((tab.shape[1],) + bucket.shape, F32)
    for b in range(N_BUCKETS):
        out = out + jnp.where(bucket[None] == b, tab[b].reshape((-1,) + (1,) * bucket.ndim), 0.0)
    return out


def _diff_bias_tiles(table, t):
    assert t >= MAX_DISTANCE
    r = jnp.arange(t)[:, None]
    c = jnp.arange(t)[None, :]
    tiles = []
    for cls in range(3):
        d = cls * t + r - c
        b = _t5_bucket_bias(table, d)
        if cls == 0:
            b = jnp.where(d >= 0, b, NEG_BIG)
        tiles.append(b)
    return jnp.stack(tiles, axis=1)


def _memkv_kernel(m_ref, w_ref, gmem_ref, gk_ref, k_o, v_o, kb_o, vb_o):
    x = m_ref[...]
    r = _rsqrt_mean(jnp.sum(x * x, axis=-1, keepdims=True), D_MODEL)
    h = (x * r * gmem_ref[...]).astype(BF)
    kv = jnp.dot(h, w_ref[...], preferred_element_type=F32)
    gk = gk_ref[...]
    nk = MEM_HEADS * MEM_HD
    tm = x.shape[0]
    v = kv[:, nk:]
    for hh in range(MEM_HEADS):
        z = kv[:, hh * LANES:(hh + 1) * LANES]
        rr = _rsqrt_mean(jnp.sum(z * z, axis=-1, keepdims=True), MEM_HD)
        y = z * rr * gk
        k_o[pl.ds(hh, tm, stride=MEM_HEADS), :] = y
        v_o[pl.ds(hh, tm, stride=MEM_HEADS), :] = v[:, hh * LANES:(hh + 1) * LANES]
        kb_o[:, hh * LANES:(hh + 1) * LANES] = y.astype(BF)
    vb_o[...] = v.astype(BF)


def _memory_kv(mem2d, w_mem_kv, g_mem, g_k_mem, tm):
    t = mem2d.shape[0]
    nk = MEM_HEADS * MEM_HD
    row = lambda w: pl.BlockSpec((tm, w), lambda i: (i, 0))
    w = w_mem_kv.astype(BF)
    return pl.pallas_call(
        _memkv_kernel,
        grid=(t // tm,),
        in_specs=[row(D_MODEL), _const_spec(w.shape), _const_spec((1, D_MODEL)), _const_spec((1, MEM_HD))],
        out_specs=[pl.BlockSpec((tm * MEM_HEADS, MEM_HD), lambda i: (i, 0))] * 2 + [row(nk)] * 2,
        out_shape=[jax.ShapeDtypeStruct((t * MEM_HEADS, MEM_HD), F32)] * 2 + [jax.ShapeDtypeStruct((t, nk), BF)] * 2,
        compiler_params=_cparams(("parallel",)),
        name="mem_kv",
    )(mem2d, w, g_mem.reshape(1, -1), g_k_mem.reshape(1, -1))


def _merge_kernel(lam_ref, x_ref, omla_ref, odiff_ref, omem_ref, wg_ref, wmla_ref, wdiff_ref, wmem_ref, wout_ref,
                  gattn_ref, gsub_ref, gffn_ref, wr_ref, br_ref, x2_o, h2_o, topi_o, gatew_o):
    x = x_ref[...]
    r = _rsqrt_mean(jnp.sum(x * x, axis=-1, keepdims=True), D_MODEL)
    h = (x * r * gattn_ref[...]).astype(BF)
    lam = lam_ref[0]

    def gate(b):
        return jax.nn.sigmoid(jnp.dot(h, wg_ref[:, b * D_MODEL:(b + 1) * D_MODEL], preferred_element_type=F32))

    mixed = gate(0) * jnp.dot(omla_ref[...], wmla_ref[...], preferred_element_type=F32)

    od = odiff_ref[...]
    gsub = gsub_ref[...]
    bdiff = None
    for hh in range(DIFF_HEADS):
        o = od[:, (2 * hh) * LANES:(2 * hh + 1) * LANES] - lam * od[:, (2 * hh + 1) * LANES:(2 * hh + 2) * LANES]
        rr = _rsqrt_mean(jnp.sum(o * o, axis=-1, keepdims=True), DIFF_VD)
        o = (o * rr * gsub * (1.0 - LAM_INIT)).astype(BF)
        part = jnp.dot(o, wdiff_ref[hh * LANES:(hh + 1) * LANES, :], preferred_element_type=F32)
        bdiff = part if bdiff is None else bdiff + part
    mixed = mixed + gate(1) * bdiff
    mixed = mixed + gate(2) * jnp.dot(omem_ref[...], wmem_ref[...], preferred_element_type=F32)
    x2 = x + jnp.dot(mixed.astype(BF), wout_ref[...], preferred_element_type=F32)
    x2_o[...] = x2

    r2 = _rsqrt_mean(jnp.sum(x2 * x2, axis=-1, keepdims=True), D_MODEL)
    h2 = x2 * r2 * gffn_ref[...]
    h2_o[...] = h2
    h_hi = h2.astype(BF)
    h_lo = (h2 - h_hi.astype(F32)).astype(BF)
    both = jnp.dot(h_hi, wr_ref[...], preferred_element_type=F32)
    logits = (both[:, :LANES] + both[:, LANES:]
              + jnp.dot(h_lo, wr_ref[:, :LANES], preferred_element_type=F32) + br_ref[...])
    lane = lax.broadcasted_iota(jnp.int32, logits.shape, 1)
    vals, idxs = [], []
    l = logits
    for _ in range(TOP_K):
        mx = jnp.max(l, axis=-1, keepdims=True)
        idx = jnp.min(jnp.where(l == mx, lane, LANES), axis=-1, keepdims=True)
        vals.append(mx)
        idxs.append(idx)
        l = jnp.where(lane == idx, -jnp.inf, l)
    es = [jnp.exp(v - vals[0]) for v in vals]
    tot = es[0] + es[1] + es[2] + es[3]
    ti = jnp.zeros(logits.shape, jnp.int32)
    gw = jnp.zeros(logits.shape, F32)
    for k in range(TOP_K):
        ti = jnp.where(lane == k, idxs[k], ti)
        gw = jnp.where(lane == k, es[k] / tot, gw)
    topi_o[...] = ti
    gatew_o[...] = gw


def _merge(x2d, o_mla, o_diff, o_mem, lam, mw, tm):
    t = x2d.shape[0]
    row = lambda w: pl.BlockSpec((tm, w), lambda i: (i, 0))
    wnames = ["wg", "wmla", "wdiff", "wmem", "wout", "gattn", "gsub", "gffn", "wr", "br"]
    ws = [mw[k] for k in wnames]
    return pl.pallas_call(
        _merge_kernel,
        grid=(t // tm,),
        in_specs=[pl.BlockSpec(memory_space=pltpu.SMEM), row(D_MODEL), row(o_mla.shape[1]), row(o_diff.shape[1]),
                  row(o_mem.shape[1])] + [_const_spec(w.shape) for w in ws],
        out_specs=[row(D_MODEL), row(D_MODEL), row(LANES), row(LANES)],
        out_shape=[jax.ShapeDtypeStruct((t, D_MODEL), F32), jax.ShapeDtypeStruct((t, D_MODEL), F32),
                   jax.ShapeDtypeStruct((t, LANES), jnp.int32), jax.ShapeDtypeStruct((t, LANES), F32)],
        compiler_params=_cparams(("parallel",)),
        name="merge",
    )(lam, x2d, o_mla, o_diff, o_mem, *ws)


def _dispatch_kernel(pend_ref, dest_ref, h_ref, xs_hbm, zero_sc, sem, *, tm, tmg):
    i = pl.program_id(0)
    n_tiles = xs_hbm.shape[0] // tmg

    @pl.when(i == 0)
    def _():
        zero_sc[...] = jnp.zeros(zero_sc.shape, F32)

        def tail(e):
            end = pend_ref[e + 1]
            return pl.multiple_of(end - tmg, tmg), end > pend_ref[e]

        for e in range(N_EXPERTS):
            start, nonempty = tail(e)

            @pl.when(nonempty)
            def _():
                pltpu.make_async_copy(zero_sc, xs_hbm.at[pl.ds(start, tmg)], sem).start()
        for e in range(N_EXPERTS):
            start, nonempty = tail(e)

            @pl.when(nonempty)
            def _():
                pltpu.make_async_copy(zero_sc, xs_hbm.at[pl.ds(start, tmg)], sem).wait()

        def unused(tile):
            return pltpu.make_async_copy(zero_sc, xs_hbm.at[pl.ds(pl.multiple_of(tile * tmg, tmg), tmg)], sem)

        first_unused = pend_ref[N_EXPERTS] // tmg
        lax.fori_loop(first_unused, n_tiles, lambda tile, c: (unused(tile).start(), c)[1], 0)
        lax.fori_loop(first_unused, n_tiles, lambda tile, c: (unused(tile).wait(), c)[1], 0)

    def issue(rb, c):
        for u in range(SUBLANES):
            for k in range(TOP_K):
                dst = xs_hbm.at[pl.ds(dest_ref[0, 0, (rb * SUBLANES + u) * TOP_K + k], 1)]
                pltpu.make_async_copy(h_ref.at[rb, pl.ds(u, 1)], dst, sem).start(priority=k % 2)
        return c

    lax.fori_loop(0, tm // SUBLANES, issue, 0)
    for _ in range(TOP_K):
        pltpu.make_async_copy(xs_hbm.at[pl.ds(0, tm)], xs_hbm.at[pl.ds(0, tm)], sem).wait()


def _dispatch(h2, dest, pend0, n_rows, tm, tmg):
    t = h2.shape[0]
    dest3 = dest.reshape(t // tm, 1, tm * TOP_K)
    return pl.pallas_call(
        functools.partial(_dispatch_kernel, tm=tm, tmg=tmg),
        grid_spec=pltpu.PrefetchScalarGridSpec(
            num_scalar_prefetch=1,
            grid=(t // tm,),
            in_specs=[pl.BlockSpec((1, 1, tm * TOP_K), lambda i, p: (i, 0, 0), memory_space=pltpu.SMEM),
                      pl.BlockSpec((tm // SUBLANES, SUBLANES, D_MODEL), lambda i, p: (i, 0, 0))],
            out_specs=pl.BlockSpec(memory_space=pl.ANY),
            scratch_shapes=[pltpu.VMEM((tmg, D_MODEL), F32), pltpu.SemaphoreType.DMA(())],
        ),
        out_shape=jax.ShapeDtypeStruct((n_rows, D_MODEL), F32),
        compiler_params=_cparams(("arbitrary",)),
        name="moe_dispatch",
    )(pend0, dest3, h2.reshape(t // SUBLANES, SUBLANES, D_MODEL))


def _experts_kernel(te_ref, nu_ref, x_ref, wgu_ref, bgu_ref, wd_ref, bd_ref, o_ref):
    i = pl.program_id(0)

    @pl.when(i < nu_ref[0])
    def _():
        x = x_ref[...].astype(BF)
        gu = jnp.dot(x, wgu_ref[...], preferred_element_type=F32) + bgu_ref[...]
        gate = jnp.minimum(gu[:, :D_EXPERT], SWIGLU_LIMIT)
        up = jnp.clip(gu[:, D_EXPERT:], -SWIGLU_LIMIT, SWIGLU_LIMIT)
        act = (up + 1.0) * (gate * jax.nn.sigmoid(SWIGLU_ALPHA * gate))
        o_ref[...] = jnp.dot(act.astype(BF), wd_ref[...], preferred_element_type=F32) + bd_ref[...]

    @pl.when(i >= nu_ref[0])
    def _():
        o_ref[...] = jnp.zeros(o_ref.shape, F32)


def _experts(xs, tile_expert, n_used, wgu, bgu, wd, bd, tmg):
    n_rows = xs.shape[0]
    n_tiles = n_rows // tmg
    return pl.pallas_call(
        _experts_kernel,
        grid_spec=pltpu.PrefetchScalarGridSpec(
            num_scalar_prefetch=2,
            grid=(n_tiles,),
            in_specs=[pl.BlockSpec((tmg, D_MODEL), lambda i, te, nu: (jnp.minimum(i, nu[0] - 1), 0)),
                      pl.BlockSpec((None, D_MODEL, 2 * D_EXPERT), lambda i, te, nu: (te[i], 0, 0)),
                      pl.BlockSpec((None, 1, 2 * D_EXPERT), lambda i, te, nu: (te[i], 0, 0)),
                      pl.BlockSpec((None, D_EXPERT, D_MODEL), lambda i, te, nu: (te[i], 0, 0)),
                      pl.BlockSpec((None, 1, D_MODEL), lambda i, te, nu: (te[i], 0, 0))],
            out_specs=pl.BlockSpec((tmg, D_MODEL), lambda i, te, nu: (i, 0)),
        ),
        out_shape=jax.ShapeDtypeStruct((n_rows, D_MODEL), F32),
        compiler_params=_cparams(("arbitrary",)),
        name="moe_experts",
    )(tile_expert, n_used, xs, wgu, bgu, wd, bd)


def _combine_kernel(dest_ref, destn_ref, x2_ref, gw_ref, out_hbm, y_ref, buf, sem, *, tm):
    i = pl.program_id(0)
    n = pl.num_programs(0)

    def gather(idx_ref, slot):
        def issue(rb, c):
            for u in range(SUBLANES):
                for k in range(TOP_K):
                    src = out_hbm.at[pl.ds(idx_ref[0, 0, (rb * SUBLANES + u) * TOP_K + k], 1)]
                    pltpu.make_async_copy(src, buf.at[slot, k, rb, pl.ds(u, 1)],
                                          sem.at[slot]).start(priority=k % 2)
            return c

        lax.fori_loop(0, tm // SUBLANES, issue, 0)

    slot = i % 2

    @pl.when(i == 0)
    def _():
        gather(dest_ref, 0)

    @pl.when(i + 1 < n)
    def _():
        gather(destn_ref, 1 - slot)

    for k in range(TOP_K):
        pltpu.make_async_copy(out_hbm.at[pl.ds(0, tm)], out_hbm.at[pl.ds(0, tm)], sem.at[slot]).wait()
    gw = gw_ref[...]
    y = x2_ref[...]
    for k in range(TOP_K):
        y = y + gw[:, k:k + 1] * buf[slot, k].reshape(tm, D_MODEL)
    y_ref[...] = y


def _combine(x2, gate_w, dest, out_rows, tm):
    t = x2.shape[0]
    nt = t // tm
    dest3 = dest.reshape(nt, 1, tm * TOP_K)
    idx = lambda f: pl.BlockSpec((1, 1, tm * TOP_K), f, memory_space=pltpu.SMEM)
    return pl.pallas_call(
        functools.partial(_combine_kernel, tm=tm),
        grid=(nt,),
        in_specs=[idx(lambda i: (i, 0, 0)), idx(lambda i: (jnp.minimum(i + 1, nt - 1), 0, 0)),
                  pl.BlockSpec((tm, D_MODEL), lambda i: (i, 0)),
                  pl.BlockSpec((tm, LANES), lambda i: (i, 0)),
                  pl.BlockSpec(memory_space=pl.ANY)],
        out_specs=pl.BlockSpec((tm, D_MODEL), lambda i: (i, 0)),
        out_shape=jax.ShapeDtypeStruct((t, D_MODEL), F32),
        scratch_shapes=[pltpu.VMEM((2, TOP_K, tm // SUBLANES, SUBLANES, D_MODEL), F32),
                        pltpu.SemaphoreType.DMA((2,))],
        compiler_params=_cparams(("arbitrary",)),
        name="moe_combine",
    )(dest3, dest3, x2, gate_w, out_rows)


def _route(topi, tmg, n_tiles):
    e_ids = jnp.arange(N_EXPERTS, dtype=jnp.int32)
    hit = (topi[:, :, None] == e_ids[None, None, :])
    onehot = jnp.any(hit, axis=1).astype(jnp.int32)
    counts = jnp.sum(onehot, axis=0)
    pos = jnp.cumsum(onehot, axis=0) - onehot
    padded = (counts + tmg - 1) // tmg * tmg
    pend = jnp.cumsum(padded)
    pstart = pend - padded
    base = pstart[None, :] + pos
    dest = jnp.sum(jnp.where(hit, base[:, None, :], 0), axis=-1).astype(jnp.int32)
    n_used = (pend[-1] // tmg).astype(jnp.int32).reshape(1)
    tile_row0 = jnp.arange(n_tiles, dtype=jnp.int32) * tmg
    tile_expert = jnp.minimum(jnp.sum((pend[None, :] <= tile_row0[:, None]).astype(jnp.int32), axis=1),
                              N_EXPERTS - 1).astype(jnp.int32)
    pend0 = jnp.concatenate([jnp.zeros((1,), jnp.int32), pend.astype(jnp.int32)])
    return dest, pend0, n_used, tile_expert


def _moe(x2, h2, topi_pad, gate_w, ew, tm_tok, tmg):
    t = x2.shape[0]
    n_tiles = (t * TOP_K + N_EXPERTS * (tmg - 1) + tmg - 1) // tmg
    dest, pend0, n_used, tile_expert = _route(topi_pad[:, :TOP_K], tmg, n_tiles)
    xs = _dispatch(h2, dest.reshape(-1), pend0, n_tiles * tmg, tm_tok, tmg)
    out_rows = _experts(xs, tile_expert, n_used, ew["wgu"], ew["bgu"], ew["wd"], ew["bd"], tmg)
    return _combine(x2, gate_w, dest.reshape(-1), out_rows, tm_tok)


def _absorb_kernel(qm_ref, wukt_ref, o_ref):
    for hh in range(MLA_HEADS):
        qn = qm_ref[:, hh * LANES:hh * LANES + MLA_NOPE]
        o_ref[hh] = jnp.dot(qn, wukt_ref[hh], preferred_element_type=F32).astype(BF)


def _absorb(qm, wukt):
    ns = qm.shape[0]
    return pl.pallas_call(
        _absorb_kernel,
        in_specs=[_const_spec(qm.shape), _const_spec(wukt.shape)],
        out_specs=_const_spec((MLA_HEADS, ns, KV_RANK)),
        out_shape=jax.ShapeDtypeStruct((MLA_HEADS, ns, KV_RANK), BF),
        grid=(1,),
        compiler_params=_cparams(("arbitrary",)),
        name="sample_q_absorb",
    )(qm, wukt)


def _paged_kernel(pt_ref, qlat_ref, qr_ref, qbd_ref, bias_ref, *rest, pb):
    lat_refs = rest[0:pb]
    krt_refs = rest[pb:2 * pb]
    kt_refs = rest[2 * pb:3 * pb]
    v_refs = rest[3 * pb:4 * pb]
    oa_ref, od_ref, ma, la, acca, md, ld, accd = rest[4 * pb:]
    j = pl.program_id(1)

    @pl.when(j == 0)
    def _():
        ma[...] = jnp.full(ma.shape, -jnp.inf, F32)
        la[...] = jnp.zeros(la.shape, F32)
        acca[...] = jnp.zeros(acca.shape, F32)
        md[...] = jnp.full(md.shape, -jnp.inf, F32)
        ld[...] = jnp.zeros(ld.shape, F32)
        accd[...] = jnp.zeros(accd.shape, F32)

    qlat = qlat_ref[...]
    qr = qr_ref[...]
    qbd = qbd_ref[...]
    rowh = lax.broadcasted_iota(jnp.int32, (2 * DIFF_HEADS, LANES), 0) // 2
    page = lambda a, p: a[:, p * PAGE_SIZE:(p + 1) * PAGE_SIZE]

    s_a = jnp.concatenate(
        [lax.dot_general(qlat, lat_refs[p][...].astype(BF), (((1,), (1,)), ((), ())), preferred_element_type=F32)
         + jnp.dot(qr, krt_refs[p][...].astype(BF), preferred_element_type=F32) for p in range(pb)], axis=1)
    s_d = jnp.concatenate([jnp.dot(qbd, kt_refs[p][...].astype(BF), preferred_element_type=F32) for p in range(pb)],
                          axis=1) + bias_ref[...]

    def softmax_update(s, m_ref, l_ref):
        m_new = jnp.maximum(m_ref[...], jnp.max(s, axis=-1, keepdims=True))
        alpha = jnp.exp(m_ref[...] - m_new)
        pe = jnp.exp(s - m_new)
        l_ref[...] = alpha * l_ref[...] + jnp.sum(pe, axis=-1, keepdims=True)
        m_ref[...] = m_new
        return alpha, pe.astype(BF)

    alpha_a, pe_a = softmax_update(s_a, ma, la)
    alpha_d, pe_d = softmax_update(s_d, md, ld)

    pv = jnp.dot(page(pe_a, 0), lat_refs[0][...].astype(BF), preferred_element_type=F32)
    for p in range(1, pb):
        pv = pv + jnp.dot(page(pe_a, p), lat_refs[p][...].astype(BF), preferred_element_type=F32)
    acca[...] = alpha_a * acca[...] + pv

    pv = jnp.zeros((2 * DIFF_HEADS, DIFF_VD), F32)
    for hh in range(DIFF_HEADS):
        pvh = None
        for p in range(pb):
            vh = v_refs[p][pl.ds(hh, PAGE_SIZE, stride=DIFF_HEADS), :].astype(BF)
            part = jnp.dot(page(pe_d, p), vh, preferred_element_type=F32)
            pvh = part if pvh is None else pvh + part
        pv = jnp.where(rowh == hh, pvh, pv)
    accd[...] = alpha_d * accd[...] + pv

    @pl.when(j == pl.num_programs(1) - 1)
    def _():
        oa_ref[:, 0:KV_RANK] = acca[...]
        oa_ref[:, KV_RANK:KV_RANK + LANES] = jnp.broadcast_to(ma[...], (MLA_HEADS, LANES))
        oa_ref[:, KV_RANK + LANES:] = jnp.broadcast_to(la[...], (MLA_HEADS, LANES))
        od_ref[:, 0:DIFF_VD] = accd[...]
        od_ref[:, DIFF_VD:2 * DIFF_VD] = jnp.broadcast_to(md[...], (2 * DIFF_HEADS, LANES))
        od_ref[:, 2 * DIFF_VD:] = jnp.broadcast_to(ld[...], (2 * DIFF_HEADS, LANES))


def _paged_attention(page_table, qlat, qr, qbd, bias, lat_pages, krt_pages, kt_pages, v_pages, pb):
    ns, n_pages = page_table.shape
    assert n_pages % pb == 0

    def page_spec(shape, p):
        return pl.BlockSpec((None,) + shape, lambda n, j, pt: (pt[n, j * pb + p], 0, 0))

    per_n = lambda shape: pl.BlockSpec((None,) + shape, lambda n, j, pt: (n, 0, 0))
    in_specs = [per_n((MLA_HEADS, KV_RANK)), per_n((MLA_HEADS, MLA_ROPE)), per_n((2 * DIFF_HEADS, 512)),
                pl.BlockSpec((2 * DIFF_HEADS, pb * PAGE_SIZE), lambda n, j, pt: (0, j))]
    ins = [qlat, qr, qbd, bias]
    for arr, shape in ((lat_pages, (PAGE_SIZE, KV_RANK)), (krt_pages, (MLA_ROPE, PAGE_SIZE)),
                       (kt_pages, (512, PAGE_SIZE)), (v_pages, (512, DIFF_VD))):
        for p in range(pb):
            in_specs.append(page_spec(shape, p))
            ins.append(arr)
    wa = KV_RANK + 2 * LANES
    wd = 3 * DIFF_VD
    return pl.pallas_call(
        functools.partial(_paged_kernel, pb=pb),
        grid_spec=pltpu.PrefetchScalarGridSpec(
            num_scalar_prefetch=1,
            grid=(ns, n_pages // pb),
            in_specs=in_specs,
            out_specs=[per_n((MLA_HEADS, wa)), per_n((2 * DIFF_HEADS, wd))],
            scratch_shapes=[pltpu.VMEM((MLA_HEADS, 1), F32), pltpu.VMEM((MLA_HEADS, 1), F32),
                            pltpu.VMEM((MLA_HEADS, KV_RANK), F32),
                            pltpu.VMEM((2 * DIFF_HEADS, 1), F32), pltpu.VMEM((2 * DIFF_HEADS, 1), F32),
                            pltpu.VMEM((2 * DIFF_HEADS, DIFF_VD), F32)],
        ),
        out_shape=[jax.ShapeDtypeStruct((ns, MLA_HEADS, wa), F32),
                   jax.ShapeDtypeStruct((ns, 2 * DIFF_HEADS, wd), F32)],
        compiler_params=_cparams(("parallel", "arbitrary")),
        name="sample_paged_attn",
    )(page_table, *ins)


def _finish_kernel(b0_ref, pa_ref, pd_ref, qlat_ref, qm_ref, qd_ref, ckv_ref, kr_ref, kd_ref, vd_ref, wuv_ref,
                   omla_o, odiff_o):
    ckv = ckv_ref[...]
    kr = kr_ref[...]
    krs = pltpu.roll(kr, MLA_NOPE, 1)
    lane = lax.broadcasted_iota(jnp.int32, (1, LANES), 1)
    rope_lanes = (lane >= MLA_NOPE) & (lane < MLA_QK)
    for hh in range(MLA_HEADS):
        acc = pa_ref[hh, :, 0:KV_RANK]
        m = pa_ref[hh, :, KV_RANK:KV_RANK + 1]
        l = pa_ref[hh, :, KV_RANK + LANES:KV_RANK + LANES + 1]
        qg = qm_ref[:, hh * LANES:(hh + 1) * LANES].astype(F32)
        s = (jnp.sum(qlat_ref[hh].astype(F32) * ckv, axis=-1, keepdims=True)
             + jnp.sum(jnp.where(rope_lanes, qg * krs, 0.0), axis=-1, keepdims=True))
        m_new = jnp.maximum(m, s)
        alpha = jnp.exp(m - m_new)
        pn = jnp.exp(s - m_new)
        l = alpha * l + pn
        acc = alpha * acc + pn * ckv
        o_lat = (acc / l).astype(BF)
        omla_o[:, hh * MLA_VD:(hh + 1) * MLA_VD] = jnp.dot(
            o_lat, wuv_ref[:, hh * MLA_VD:(hh + 1) * MLA_VD], preferred_element_type=F32).astype(BF)
    for g in range(2 * DIFF_HEADS):
        hh = g // 2
        acc = pd_ref[g, :, 0:DIFF_VD]
        m = pd_ref[g, :, DIFF_VD:DIFF_VD + 1]
        l = pd_ref[g, :, 2 * DIFF_VD:2 * DIFF_VD + 1]
        qg = qd_ref[:, g * LANES:(g + 1) * LANES].astype(F32)
        s = jnp.sum(qg * kd_ref[:, hh * LANES:(hh + 1) * LANES], axis=-1, keepdims=True) + b0_ref[hh]
        m_new = jnp.maximum(m, s)
        alpha = jnp.exp(m - m_new)
        pn = jnp.exp(s - m_new)
        l = alpha * l + pn
        acc = alpha * acc + pn * vd_ref[:, hh * LANES:(hh + 1) * LANES]
        odiff_o[:, g * LANES:(g + 1) * LANES] = acc / l


def _finish(b0, pa, pd, qlat, qm, qd, ckv, kr128, kd, vd, wuv):
    ns = qm.shape[0]
    ins = [pa, pd, qlat, qm, qd, ckv, kr128, kd, vd, wuv]
    return pl.pallas_call(
        _finish_kernel,
        grid=(1,),
        in_specs=[pl.BlockSpec(memory_space=pltpu.SMEM)] + [_const_spec(a.shape) for a in ins],
        out_specs=[_const_spec((ns, MLA_HEADS * MLA_VD)), _const_spec((ns, 2 * DIFF_HEADS * LANES))],
        out_shape=[jax.ShapeDtypeStruct((ns, MLA_HEADS * MLA_VD), BF),
                   jax.ShapeDtypeStruct((ns, 2 * DIFF_HEADS * LANES), F32)],
        compiler_params=_cparams(("arbitrary",)),
        name="sample_attn_finish",
    )(b0, *ins)


def _mem_decode_kernel(q_ref, k_ref, v_ref, o_ref, *, m_len):
    q = q_ref[...]
    rows = lax.broadcasted_iota(jnp.int32, (8, LANES), 0)
    out = jnp.zeros((8, LANES), F32)
    for hh in range(MEM_HEADS):
        kh = k_ref[pl.ds(hh, m_len, stride=MEM_HEADS), :].astype(BF)
        vh = v_ref[pl.ds(hh, m_len, stride=MEM_HEADS), :].astype(BF)
        s = lax.dot_general(q, kh, (((1,), (1,)), ((), ())), preferred_element_type=F32)
        m = jnp.max(s, axis=-1, keepdims=True)
        pe = jnp.exp(s - m)
        l = jnp.sum(pe, axis=-1, keepdims=True)
        o = jnp.dot(pe.astype(BF), vh, preferred_element_type=F32) / l
        out = jnp.where(rows == hh, o, out)
    o_ref[...] = out.astype(o_ref.dtype)


def _mem_decode(q8, mem_k, mem_v, m_len):
    ns = q8.shape[0]
    blk = lambda r: pl.BlockSpec((None, r, LANES), lambda n: (n, 0, 0))
    return pl.pallas_call(
        functools.partial(_mem_decode_kernel, m_len=m_len),
        grid=(ns,),
        in_specs=[blk(8), blk(m_len * MEM_HEADS), blk(m_len * MEM_HEADS)],
        out_specs=blk(8),
        out_shape=jax.ShapeDtypeStruct((ns, 8, LANES), BF),
        compiler_params=_cparams(("parallel",)),
        name="sample_mem_attn",
    )(q8, mem_k, mem_v)


def _sample_path(x_sample, caches, page_table, t5_bias, pw, lam, w_uk, w_uv):
    cache_lat, cache_kr, cache_dk, cache_dv, cache_mk, cache_mv = caches
    ns, t1, d = x_sample.shape
    assert t1 == 1, "one new token per sample"
    n_pages = page_table.shape[1]
    pos = jnp.full((ns,), n_pages * PAGE_SIZE, jnp.int32)
    (qm, ckv, kr, qd, kd, kdb, vd, vdb, qmem) = _project(
        x_sample.reshape(ns, d), _rope_lane_tables(pos), pw["wp"], pw["gains"], ns, False)
    n_pool = cache_lat.shape[1]
    lat_pages = cache_lat[0]
    krt_pages = jnp.transpose(cache_kr[0], (0, 2, 1))
    kt_pages = jnp.transpose(cache_dk[0], (0, 2, 3, 4, 1)).reshape(n_pool, 512, PAGE_SIZE)
    v_pages = cache_dv[0].reshape(n_pool, PAGE_SIZE * DIFF_HEADS, DIFF_VD)
    m_len = cache_mk.shape[2]
    mem_k = cache_mk[0].reshape(ns, m_len * MEM_HEADS, MEM_HD)
    mem_v = cache_mv[0].reshape(ns, m_len * MEM_HEADS, MEM_HD)

    wukt = jnp.transpose(w_uk, (1, 2, 0)).astype(BF)
    qlat_h = _absorb(qm, wukt)
    qlat = jnp.transpose(qlat_h, (1, 0, 2))
    qr = qm.reshape(ns, MLA_HEADS, LANES)[:, :, MLA_NOPE:MLA_QK]
    qd4 = qd.reshape(ns, DIFF_HEADS, 2, LANES)
    z = jnp.zeros_like(qd4)
    qbd = jnp.stack([jnp.where(jnp.arange(DIFF_HEADS)[None, :, None, None] == hh, qd4, z)
                     for hh in range(DIFF_HEADS)], axis=3)
    qbd = qbd.reshape(ns, 2 * DIFF_HEADS, DIFF_HEADS * LANES)
    q_pos = n_pages * PAGE_SIZE
    bias = _t5_bucket_bias(t5_bias, q_pos - jnp.arange(n_pages * PAGE_SIZE))
    bias = jnp.repeat(bias, 2, axis=0)
    b0 = _t5_bucket_bias(t5_bias, jnp.zeros((1,), jnp.int32))[:, 0]
    pa, pd = _paged_attention(page_table, qlat, qr, qbd, bias, lat_pages, krt_pages, kt_pages, v_pages, pw["pb"])
    kr128 = jnp.pad(kr, ((0, 0), (0, LANES - MLA_ROPE)))
    o_mla, o_diff = _finish(b0, jnp.transpose(pa, (1, 0, 2)), jnp.transpose(pd, (1, 0, 2)), qlat_h, qm, qd, ckv,
                            kr128, kd, vd.reshape(ns, DIFF_HEADS * DIFF_VD), pw["wuv_c"])
    q8 = jnp.pad(qmem.reshape(ns, MEM_HEADS, MEM_HD), ((0, 0), (0, 8 - MEM_HEADS), (0, 0)))
    o_mem = _mem_decode(q8, mem_k, mem_v, m_len)[:, :MEM_HEADS].reshape(ns, MEM_HEADS * MEM_HD)
    x2, h2, topi, gw = _merge(x_sample.reshape(ns, d), o_mla, o_diff, o_mem, lam, pw["merge"], ns)
    y = _moe(x2, h2, topi, gw, pw["experts"], ns, pw["tmg_sample"])
    return (y.reshape(ns, 1, d), ckv.reshape(1, ns, 1, KV_RANK), kr.reshape(1, ns, 1, MLA_ROPE),
            kd.reshape(1, ns, 1, DIFF_HEADS, 2, DIFF_HD), vd.reshape(1, ns, 1, DIFF_HEADS, DIFF_VD))


def _prompt_path(x_prompt, mem_prompt, t5_bias, pw, lam):
    n, s, d = x_prompt.shape
    t = n * s
    pos_tab = _rope_lane_tables(jnp.arange(s))
    (qm, ckv, kr, qd, kd, kdb, vd, vdb, qmem, kmla, vmla) = _project(
        x_prompt.reshape(t, d), pos_tab, pw["wp"], pw["gains"], pw["tm_proj"], True, pw["wuk"], pw["wuv"])
    tq = pw["tq"]
    r3 = lambda a: a.reshape(n, s, a.shape[-1])
    o_mla = _flash(r3(qm), r3(kmla), r3(vmla), None, causal=True, kv_shared=False, hps=2, tq=tq, tk=tq,
                   out_dtype=BF, name="attn_mla", pack64=True)
    o_diff = _flash(r3(qd), r3(kdb), r3(vdb), _diff_bias_tiles(t5_bias, tq), causal=True, kv_shared=True, hps=2,
                    tq=tq, tk=tq, out_dtype=F32, name="attn_diff")
    m_len = mem_prompt.shape[1]
    mk, mv, mkb, mvb = _memory_kv(mem_prompt.reshape(n * m_len, d), pw["w_mem_kv"], pw["g_mem"], pw["g_k_mem"],
                                  min(512, n * m_len))
    rm = lambda a: a.reshape(n, m_len, a.shape[-1])
    o_mem = _flash(r3(qmem), rm(mkb), rm(mvb), None, causal=False, kv_shared=False, hps=2, tq=tq, tk=m_len,
                   out_dtype=BF, name="attn_mem")
    x2, h2, topi, gw = _merge(x_prompt.reshape(t, d), o_mla.reshape(t, -1), o_diff.reshape(t, -1),
                              o_mem.reshape(t, -1), lam, pw["merge"], pw["tm_merge"])
    y = _moe(x2, h2, topi, gw, pw["experts"], pw["tm_tok"], pw["tmg"])
    outs = (y.reshape(n, s, d), ckv.reshape(1, n, s, KV_RANK), jnp.transpose(kr, (0, 2, 1))[None],
            jnp.transpose(kd.reshape(1, n, DIFF_HEADS, 2, DIFF_HD, s), (0, 1, 5, 2, 3, 4)),
            vd.reshape(1, n, s, DIFF_HEADS, DIFF_VD),
            mk.reshape(1, n, m_len, MEM_HEADS, MEM_HD), mv.reshape(1, n, m_len, MEM_HEADS, MEM_HD))
    return outs


def _prepare(g_attn, w_in, g_q_mla, g_ckv, g_krope, w_uk, w_uv, g_q_diff, g_k_diff, g_subln, g_mem, w_mem_kv,
             g_q_mem, g_k_mem, w_br_mla, w_br_diff, w_br_mem, w_out, g_ffn, w_router, b_router, w_gate_up,
             b_gate_up, w_down, b_down):
    wp, wg = _pack_proj_weights(w_in, g_q_mla, g_krope)
    gains = [g_attn.reshape(1, -1),
             _pad_lanes(g_q_mla * (MLA_QK ** -0.5)),
             g_ckv.reshape(1, -1),
             _pad_lanes(g_krope),
             jnp.tile(g_q_diff * (DIFF_HD ** -0.5), 2).reshape(1, -1),
             jnp.tile(g_k_diff, 2).reshape(1, -1),
             (g_q_mem * (MEM_HD ** -0.5)).reshape(1, -1)]
    wuk = jnp.pad(w_uk, ((0, 0), (0, 0), (0, LANES - MLA_NOPE))).reshape(KV_RANK, MLA_HEADS * LANES).astype(BF)
    wuv = jnp.pad(w_uv, ((0, 0), (0, 0), (0, LANES - MLA_VD))).reshape(KV_RANK, MLA_HEADS * LANES).astype(BF)
    wmla = w_br_mla.astype(BF)
    wuv_c = w_uv.reshape(KV_RANK, MLA_HEADS * MLA_VD).astype(BF)
    wr = jnp.pad(w_router, ((0, 0), (0, LANES - N_EXPERTS)))
    wr_hi = wr.astype(BF)
    wr = jnp.concatenate([wr_hi, (wr - wr_hi.astype(F32)).astype(BF)], axis=1)
    br = jnp.concatenate([b_router, jnp.full((LANES - N_EXPERTS,), NEG_BIG, F32)]).reshape(1, LANES)
    merge = dict(wg=wg, wmla=wmla, wdiff=w_br_diff.astype(BF), wmem=w_br_mem.astype(BF), wout=w_out.astype(BF),
                 gattn=g_attn.reshape(1, -1), gsub=g_subln.reshape(1, -1), gffn=g_ffn.reshape(1, -1), wr=wr, br=br)
    experts = dict(wgu=w_gate_up.astype(BF), bgu=b_gate_up.reshape(N_EXPERTS, 1, -1), wd=w_down.astype(BF),
                   bd=b_down.reshape(N_EXPERTS, 1, -1))
    return dict(wp=wp, gains=gains, wuk=wuk, wuv=wuv, wuv_c=wuv_c, merge=merge, experts=experts, w_mem_kv=w_mem_kv,
                g_mem=g_mem, g_k_mem=g_k_mem)


def kernel(x_prompt, x_sample, mem_prompt, cache_mla_latent, cache_mla_krope, cache_diff_k, cache_diff_v,
           cache_mem_k, cache_mem_v, page_table, t5_bias, g_attn, w_in, g_q_mla, g_ckv, g_krope, w_uk, w_uv,
           g_q_diff, g_k_diff, lambda_q1, lambda_k1, lambda_q2, lambda_k2, g_subln, g_mem, w_mem_kv, g_q_mem,
           g_k_mem, w_br_mla, w_br_diff, w_br_mem, w_out, g_ffn, w_router, b_router, w_gate_up, b_gate_up,
           w_down, b_down):
    assert g_attn.shape[0] == 1, "single-layer trunk"
    l = 0
    pw = _prepare(g_attn[l], w_in[l], g_q_mla[l], g_ckv[l], g_krope[l], w_uk[l], w_uv[l], g_q_diff[l], g_k_diff[l],
                  g_subln[l], g_mem[l], w_mem_kv[l], g_q_mem[l], g_k_mem[l], w_br_mla[l], w_br_diff[l],
                  w_br_mem[l], w_out[l], g_ffn[l], w_router[l], b_router[l], w_gate_up[l], b_gate_up[l],
                  w_down[l], b_down[l])
    lam = (jnp.exp(jnp.sum(lambda_q1[l] * lambda_k1[l]).astype(F32))
           - jnp.exp(jnp.sum(lambda_q2[l] * lambda_k2[l]).astype(F32)) + LAM_INIT).reshape(1)
    s = x_prompt.shape[1]
    pw.update(tm_proj=min(TILE_PROJ, s), tq=min(TILE_ATTN, s), tm_merge=min(TILE_MERGE, s),
              tm_tok=min(TILE_TOKEN, s), tmg=TILE_EXPERT, tmg_sample=TILE_EXPERT_SAMPLE, pb=PAGES_PER_STEP)
    p = _prompt_path(x_prompt, mem_prompt, t5_bias, pw, lam)
    caches = (cache_mla_latent, cache_mla_krope, cache_diff_k, cache_diff_v, cache_mem_k, cache_mem_v)
    sm = _sample_path(x_sample, caches, page_table, t5_bias, pw, lam, w_uk[l], w_uv[l])
    return (p[0], sm[0]) + p[1:] + sm[1:]
```

```python
import functools
import math

import jax
import jax.numpy as jnp
import numpy as np
from jax import lax
from jax.experimental import pallas as pl
from jax.experimental.pallas import tpu as pltpu

D_MODEL = 1024
PAGE_SIZE = 128
MLA_HEADS = 8
MLA_NOPE = 64
MLA_ROPE = 32
MLA_QK = MLA_NOPE + MLA_ROPE
MLA_VD = 64
KV_RANK = 256
ROPE_THETA = 10000.0
DIFF_HEADS = 4
DIFF_HD = 64
DIFF_VD = 2 * DIFF_HD
MEM_HEADS = 4
MEM_HD = 128
N_BUCKETS = 32
MAX_EXACT = N_BUCKETS // 2
MAX_DISTANCE = 128
N_EXPERTS = 32
TOP_K = 4
D_EXPERT = D_MODEL
SWIGLU_LIMIT = 7.0
SWIGLU_ALPHA = 1.702
EPS = 1e-6
N_BRANCHES = 3
LAM_INIT = 0.8 - 0.6 * math.exp(-0.3 * 0)

LANES = 128
SUBLANES = 8
VMEM_LIMIT = 56 * 1024 * 1024
NEG_BIG = -1e30
TILE_PROJ = 512
TILE_ATTN = 512
TILE_MERGE = 512
TILE_TOKEN = 512

TILE_EXPERT = 512
TILE_EXPERT_SAMPLE = 128
PAGE_RING_SLOTS = 3
PAGES_PER_STEP = 16

C_QM = 0
C_CKV = C_QM + MLA_HEADS * LANES
C_KR = C_CKV + KV_RANK
C_QD = C_KR + LANES
C_KD = C_QD + 512
C_VD = C_KD + 512
C_QMEM = C_VD + 512
C_END = C_QMEM + 512

BF = jnp.bfloat16
F32 = jnp.float32


def _cparams(sem):
    return pltpu.CompilerParams(dimension_semantics=sem, vmem_limit_bytes=VMEM_LIMIT)


def _const_spec(shape):
    nd = len(shape)
    return pl.BlockSpec(shape, lambda *a: (0,) * nd)


def _rsqrt_mean(sumsq, n):
    return lax.rsqrt(sumsq * (1.0 / n) + EPS)


def _proj_kernel(x_ref, w_ref, gattn_ref, gq_ref, gckv_ref, gkr_ref, gqd_ref, gkd_ref, gqm_ref,
                 cq_ref, sq_ref, ck_ref, sk_ref, *rest, prompt):
    if prompt:
        (wuk_ref, wuv_ref, qm_o, ckv_o, kr_o, qd_o, kd_o, kdb_o, vd_o, vdb_o, qmem_o, kmla_o, vmla_o) = rest
    else:
        (qm_o, ckv_o, kr_o, qd_o, kd_o, kdb_o, vd_o, vdb_o, qmem_o) = rest
    x = x_ref[...]
    r = _rsqrt_mean(jnp.sum(x * x, axis=-1, keepdims=True), D_MODEL)
    h = (x * r * gattn_ref[...]).astype(BF)
    lane = lax.broadcasted_iota(jnp.int32, (1, LANES), 1)
    lo64 = lane < 64

    def seg(c0, n):
        return jnp.dot(h, w_ref[:, c0:c0 + n], preferred_element_type=F32)

    zq = seg(C_QM, MLA_HEADS * LANES)
    cq, sq = cq_ref[...], sq_ref[...]
    gq = gq_ref[...]
    for g in range(MLA_HEADS):
        z = zq[:, g * LANES:(g + 1) * LANES]
        zm = jnp.where(lane < MLA_QK, z, 0.0)
        rr = _rsqrt_mean(jnp.sum(zm * zm, axis=-1, keepdims=True), MLA_QK)
        y = zm * rr * gq
        yrot = pltpu.roll(z, LANES - MLA_ROPE, 1) * rr
        qm_o[:, g * LANES:(g + 1) * LANES] = (y * cq + yrot * sq).astype(BF)

    zc = seg(C_CKV, KV_RANK)
    rr = _rsqrt_mean(jnp.sum(zc * zc, axis=-1, keepdims=True), KV_RANK)
    ckv = zc * rr * gckv_ref[...]
    ckv_o[...] = ckv

    zk = seg(C_KR, LANES)
    zm = jnp.where(lane < MLA_ROPE, zk, 0.0)
    rr = _rsqrt_mean(jnp.sum(zm * zm, axis=-1, keepdims=True), MLA_ROPE)
    kr = zm * rr * gkr_ref[...] * ck_ref[...] + pltpu.roll(zk, LANES - MLA_ROPE, 1) * rr * sk_ref[...]
    if prompt:
        kr_o[...] = kr.T[:MLA_ROPE, :]
    else:
        kr_o[...] = kr[:, :MLA_ROPE]

    def halfnorm(z, gain):
        sq_ = z * z
        s_lo = jnp.sum(jnp.where(lo64, sq_, 0.0), axis=-1, keepdims=True)
        s_hi = jnp.sum(jnp.where(lo64, 0.0, sq_), axis=-1, keepdims=True)
        rr_ = jnp.where(lo64, _rsqrt_mean(s_lo, DIFF_HD), _rsqrt_mean(s_hi, DIFF_HD))
        return z * rr_ * gain

    zqd = seg(C_QD, 512)
    gqd = gqd_ref[...]
    for hh in range(DIFF_HEADS):
        y = halfnorm(zqd[:, hh * LANES:(hh + 1) * LANES], gqd)
        qd_o[:, (2 * hh) * LANES:(2 * hh + 1) * LANES] = jnp.where(lo64, y, 0.0).astype(BF)
        qd_o[:, (2 * hh + 1) * LANES:(2 * hh + 2) * LANES] = jnp.where(lo64, 0.0, y).astype(BF)

    zkd = seg(C_KD, 512)
    gkd = gkd_ref[...]
    for hh in range(DIFF_HEADS):
        y = halfnorm(zkd[:, hh * LANES:(hh + 1) * LANES], gkd)
        if prompt:
            kd_o[hh * LANES:(hh + 1) * LANES, :] = y.T
        else:
            kd_o[:, hh * LANES:(hh + 1) * LANES] = y
        kdb_o[:, hh * LANES:(hh + 1) * LANES] = y.astype(BF)

    zv = seg(C_VD, 512)
    tm = zv.shape[0]
    for hh in range(DIFF_HEADS):
        vd_o[pl.ds(hh, tm, stride=DIFF_HEADS), :] = zv[:, hh * LANES:(hh + 1) * LANES]
    vdb_o[...] = zv.astype(BF)

    zm_ = seg(C_QMEM, 512)
    gqm = gqm_ref[...]
    for hh in range(MEM_HEADS):
        z = zm_[:, hh * LANES:(hh + 1) * LANES]
        rr = _rsqrt_mean(jnp.sum(z * z, axis=-1, keepdims=True), MEM_HD)
        qmem_o[:, hh * LANES:(hh + 1) * LANES] = (z * rr * gqm).astype(BF)

    if prompt:
        cb = ckv.astype(BF)
        kn = jnp.dot(cb, wuk_ref[...], preferred_element_type=F32)
        krs = pltpu.roll(kr, MLA_NOPE, 1)
        for g in range(MLA_HEADS):
            kmla_o[:, g * LANES:(g + 1) * LANES] = (kn[:, g * LANES:(g + 1) * LANES] + krs).astype(BF)
        vmla_o[...] = jnp.dot(cb, wuv_ref[...], preferred_element_type=F32).astype(BF)


def _rope_lane_tables(pos):
    inv = ROPE_THETA ** (-jnp.arange(0, MLA_ROPE, 2, dtype=F32) / MLA_ROPE)
    ang = pos.astype(F32)[:, None] * inv[None, :]
    cos, sin = jnp.cos(ang), jnp.sin(ang)
    n = pos.shape[0]
    one = jnp.ones((n, MLA_NOPE), F32)
    z32 = jnp.zeros((n, 32), F32)
    z64 = jnp.zeros((n, 64), F32)
    cq = jnp.concatenate([one, cos, cos, z32], axis=1)
    sq = jnp.concatenate([z64, sin, sin, z32], axis=1)
    ck = jnp.concatenate([cos, cos, z32, z64], axis=1)
    sk = jnp.concatenate([sin, sin, z32, z64], axis=1)
    return cq, sq, ck, sk


def _rot_cols(w, g):
    half = MLA_ROPE // 2
    return jnp.concatenate([-w[..., half:] * g[half:], w[..., :half] * g[:half]], axis=-1)


def _pack_proj_weights(w_in, g_q_mla, g_krope):
    d = w_in.shape[0]
    cuts = np.cumsum([MLA_HEADS * MLA_QK, KV_RANK, MLA_ROPE, 512, 512, 512, 512])
    wq = w_in[:, :cuts[0]].reshape(d, MLA_HEADS, MLA_QK)
    scale = MLA_QK ** -0.5
    rotq = _rot_cols(wq[..., MLA_NOPE:], g_q_mla[MLA_NOPE:] * scale)
    wq = jnp.concatenate([wq, rotq], axis=-1).reshape(d, MLA_HEADS * LANES)
    wc = w_in[:, cuts[0]:cuts[1]]
    wk = w_in[:, cuts[1]:cuts[2]]
    wk = jnp.concatenate([wk, _rot_cols(wk, g_krope), jnp.zeros((d, 64), F32)], axis=-1)
    rest = w_in[:, cuts[2]:cuts[6]]
    wp = jnp.concatenate([wq, wc, wk, rest], axis=1).astype(BF)
    wg = w_in[:, cuts[6]:].astype(BF)
    return wp, wg


def _pad_lanes(v, n=LANES):
    return jnp.pad(v, (0, n - v.shape[0])).reshape(1, n)


def _project(x2d, pos_tab, wp, gains, tm, prompt, wuk=None, wuv=None):
    t = x2d.shape[0]
    cq, sq, ck, sk = pos_tab
    nper = cq.shape[0] // tm
    row = lambda w: pl.BlockSpec((tm, w), lambda i: (i, 0))
    tab = pl.BlockSpec((tm, LANES), lambda i: (i % nper, 0))
    ins = [x2d, wp] + list(gains) + [cq, sq, ck, sk]
    in_specs = ([row(D_MODEL), _const_spec(wp.shape)] + [_const_spec(g.shape) for g in gains] + [tab] * 4)
    rows = lambda w, dt: (jax.ShapeDtypeStruct((t, w), dt), row(w))
    if prompt:
        nb = t // cq.shape[0]
        tr = lambda w: (jax.ShapeDtypeStruct((nb, w, cq.shape[0]), F32),
                        pl.BlockSpec((None, w, tm), lambda i: (i // nper, 0, i % nper)))
        kr_out, kd_out = tr(MLA_ROPE), tr(512)
    else:
        kr_out, kd_out = rows(MLA_ROPE, F32), rows(512, F32)
    vd_out = (jax.ShapeDtypeStruct((t * DIFF_HEADS, DIFF_VD), F32),
              pl.BlockSpec((tm * DIFF_HEADS, DIFF_VD), lambda i: (i, 0)))
    outs = [rows(1024, BF), rows(KV_RANK, F32), kr_out, rows(1024, BF), kd_out, rows(512, BF), vd_out,
            rows(512, BF), rows(512, BF)]
    if prompt:
        ins += [wuk, wuv]
        in_specs += [_const_spec(wuk.shape), _const_spec(wuv.shape)]
        outs += [rows(1024, BF), rows(1024, BF)]
    return pl.pallas_call(
        functools.partial(_proj_kernel, prompt=prompt),
        grid=(t // tm,),
        in_specs=in_specs,
        out_specs=[spec for _, spec in outs],
        out_shape=[shape for shape, _ in outs],
        compiler_params=_cparams(("parallel",)),
        name="proj_prompt" if prompt else "proj_sample",
    )(*ins)


def _flash_kernel(q_ref, k_ref, v_ref, *rest, causal, has_bias, tq, tk, nk, hps, kv_shared, pack64):
    if has_bias:
        bias_ref, o_ref, m_sc, l_sc, acc_sc = rest
    else:
        o_ref, m_sc, l_sc, acc_sc = rest
    i = pl.program_id(2)
    m_sc[...] = jnp.full(m_sc.shape, -jnp.inf, F32)
    l_sc[...] = jnp.zeros(l_sc.shape, F32)
    acc_sc[...] = jnp.zeros(acc_sc.shape, F32)

    def step(j, masked):
        off = pl.multiple_of(j * tk, tk)
        scores = []
        for c in range(hps):
            kc = 0 if kv_shared else c
            q = q_ref[:, c * LANES:(c + 1) * LANES]
            k = k_ref[pl.ds(off, tk), kc * LANES:(kc + 1) * LANES]
            s = lax.dot_general(q, k, (((1,), (1,)), ((), ())), preferred_element_type=F32)
            if has_bias:
                s = s + bias_ref[jnp.minimum(i - j, 2)]
            elif masked:
                rows = lax.broadcasted_iota(jnp.int32, (tq, tk), 0)
                cols = lax.broadcasted_iota(jnp.int32, (tq, tk), 1)
                s = jnp.where(rows >= cols, s, NEG_BIG)
            scores.append(s)
        for c in range(hps):
            kc = 0 if kv_shared else c
            s = scores[c]
            v = v_ref[pl.ds(off, tk), kc * LANES:(kc + 1) * LANES]
            m_prev = m_sc[c]
            m_new = jnp.maximum(m_prev, jnp.max(s, axis=-1, keepdims=True))
            alpha = jnp.exp(m_prev - m_new)
            p = jnp.exp(s - jnp.concatenate([m_new] * (tk // LANES), axis=1))
            l_sc[c] = alpha * l_sc[c] + jnp.sum(p, axis=-1, keepdims=True)
            acc_sc[c] = alpha * acc_sc[c] + jnp.dot(p.astype(BF), v, preferred_element_type=F32)
            m_sc[c] = m_new

    if causal:
        def body(j, c):
            step(j, False)
            return c
        lax.fori_loop(0, i, body, 0)
        step(i, True)
    else:
        for j in range(nk):
            step(j, False)
    if pack64:
        lane = lax.broadcasted_iota(jnp.int32, (1, LANES), 1)
        o1 = pltpu.roll(acc_sc[1] / l_sc[1], LANES // 2, 1)
        o_ref[...] = jnp.where(lane < LANES // 2, acc_sc[0] / l_sc[0], o1).astype(o_ref.dtype)
    else:
        for c in range(hps):
            o_ref[:, c * LANES:(c + 1) * LANES] = (acc_sc[c] / l_sc[c]).astype(o_ref.dtype)


def _flash(q, k, v, bias, *, causal, kv_shared, hps, tq, tk, out_dtype, name, pack64=False):
    n, sq_len, hq = q.shape[0], q.shape[1], q.shape[2] // LANES
    sk_len = k.shape[1]
    has_bias = bias is not None
    kvw = LANES if kv_shared else hps * LANES
    in_specs = [pl.BlockSpec((None, tq, hps * LANES), lambda b, h, i: (b, i, h)),
                pl.BlockSpec((None, sk_len, kvw), lambda b, h, i: (b, 0, h)),
                pl.BlockSpec((None, sk_len, kvw), lambda b, h, i: (b, 0, h))]
    ins = [q, k, v]
    if has_bias:
        assert kv_shared and bias.shape[0] == hq // hps
        in_specs.append(pl.BlockSpec((None, 3, tq, tk), lambda b, h, i: (h, 0, 0, 0)))
        ins.append(bias)
    assert not pack64 or hps == 2
    ow = LANES if pack64 else hps * LANES
    return pl.pallas_call(
        functools.partial(_flash_kernel, causal=causal, has_bias=has_bias, tq=tq, tk=tk, nk=sk_len // tk, hps=hps,
                          kv_shared=kv_shared, pack64=pack64),
        grid=(n, hq // hps, sq_len // tq),
        in_specs=in_specs,
        out_specs=pl.BlockSpec((None, tq, ow), lambda b, h, i: (b, i, h)),
        out_shape=jax.ShapeDtypeStruct((n, sq_len, (hq // hps) * ow), out_dtype),
        scratch_shapes=[pltpu.VMEM((hps, tq, LANES), F32)] * 3,
        compiler_params=_cparams(("parallel", "parallel", "arbitrary")),
        name=name,
    )(*ins)


def _t5_bucket_bias(table, dist):
    n = jnp.maximum(dist, 0)
    nf = jnp.maximum(n, 1).astype(F32)
    large = MAX_EXACT + (jnp.log(nf / MAX_EXACT) / math.log(MAX_DISTANCE / MAX_EXACT)
                         * (N_BUCKETS - MAX_EXACT)).astype(jnp.int32)
    bucket = jnp.where(n < MAX_EXACT, n, jnp.minimum(large, N_BUCKETS - 1))
    tab = table.astype(F32)
    out = jnp.zeros((tab.shape[1],) + bucket.shape, F32)
    for b in range(N_BUCKETS):
        out = out + jnp.where(bucket[None] == b, tab[b].reshape((-1,) + (1,) * bucket.ndim), 0.0)
    return out


def _diff_bias_tiles(table, t):
    assert t >= MAX_DISTANCE
    r = jnp.arange(t)[:, None]
    c = jnp.arange(t)[None, :]
    tiles = []
    for cls in range(3):
        d = cls * t + r - c
        b = _t5_bucket_bias(table, d)
        if cls == 0:
            b = jnp.where(d >= 0, b, NEG_BIG)
        tiles.append(b)
    return jnp.stack(tiles, axis=1)


def _memkv_kernel(m_ref, w_ref, gmem_ref, gk_ref, k_o, v_o, kb_o, vb_o):
    x = m_ref[...]
    r = _rsqrt_mean(jnp.sum(x * x, axis=-1, keepdims=True), D_MODEL)
    h = (x * r * gmem_ref[...]).astype(BF)
    kv = jnp.dot(h, w_ref[...], preferred_element_type=F32)
    gk = gk_ref[...]
    nk = MEM_HEADS * MEM_HD
    tm = x.shape[0]
    v = kv[:, nk:]
    for hh in range(MEM_HEADS):
        z = kv[:, hh * LANES:(hh + 1) * LANES]
        rr = _rsqrt_mean(jnp.sum(z * z, axis=-1, keepdims=True), MEM_HD)
        y = z * rr * gk
        k_o[pl.ds(hh, tm, stride=MEM_HEADS), :] = y
        v_o[pl.ds(hh, tm, stride=MEM_HEADS), :] = v[:, hh * LANES:(hh + 1) * LANES]
        kb_o[:, hh * LANES:(hh + 1) * LANES] = y.astype(BF)
    vb_o[...] = v.astype(BF)


def _memory_kv(mem2d, w_mem_kv, g_mem, g_k_mem, tm):
    t = mem2d.shape[0]
    nk = MEM_HEADS * MEM_HD
    row = lambda w: pl.BlockSpec((tm, w), lambda i: (i, 0))
    w = w_mem_kv.astype(BF)
    return pl.pallas_call(
        _memkv_kernel,
        grid=(t // tm,),
        in_specs=[row(D_MODEL), _const_spec(w.shape), _const_spec((1, D_MODEL)), _const_spec((1, MEM_HD))],
        out_specs=[pl.BlockSpec((tm * MEM_HEADS, MEM_HD), lambda i: (i, 0))] * 2 + [row(nk)] * 2,
        out_shape=[jax.ShapeDtypeStruct((t * MEM_HEADS, MEM_HD), F32)] * 2 + [jax.ShapeDtypeStruct((t, nk), BF)] * 2,
        compiler_params=_cparams(("parallel",)),
        name="mem_kv",
    )(mem2d, w, g_mem.reshape(1, -1), g_k_mem.reshape(1, -1))


def _merge_kernel(lam_ref, x_ref, omla_ref, odiff_ref, omem_ref, wg_ref, wmla_ref, wdiff_ref, wmem_ref, wout_ref,
                  gattn_ref, gsub_ref, gffn_ref, wr_ref, br_ref, x2_o, h2_o, topi_o, gatew_o):
    x = x_ref[...]
    r = _rsqrt_mean(jnp.sum(x * x, axis=-1, keepdims=True), D_MODEL)
    h = (x * r * gattn_ref[...]).astype(BF)
    lam = lam_ref[0]

    def gate(b):
        return jax.nn.sigmoid(jnp.dot(h, wg_ref[:, b * D_MODEL:(b + 1) * D_MODEL], preferred_element_type=F32))

    mixed = gate(0) * jnp.dot(omla_ref[...], wmla_ref[...], preferred_element_type=F32)

    od = odiff_ref[...]
    gsub = gsub_ref[...]
    bdiff = None
    for hh in range(DIFF_HEADS):
        o = od[:, (2 * hh) * LANES:(2 * hh + 1) * LANES] - lam * od[:, (2 * hh + 1) * LANES:(2 * hh + 2) * LANES]
        rr = _rsqrt_mean(jnp.sum(o * o, axis=-1, keepdims=True), DIFF_VD)
        o = (o * rr * gsub * (1.0 - LAM_INIT)).astype(BF)
        part = jnp.dot(o, wdiff_ref[hh * LANES:(hh + 1) * LANES, :], preferred_element_type=F32)
        bdiff = part if bdiff is None else bdiff + part
    mixed = mixed + gate(1) * bdiff
    mixed = mixed + gate(2) * jnp.dot(omem_ref[...], wmem_ref[...], preferred_element_type=F32)
    x2 = x + jnp.dot(mixed.astype(BF), wout_ref[...], preferred_element_type=F32)
    x2_o[...] = x2

    r2 = _rsqrt_mean(jnp.sum(x2 * x2, axis=-1, keepdims=True), D_MODEL)
    h2 = x2 * r2 * gffn_ref[...]
    h2_o[...] = h2
    h_hi = h2.astype(BF)
    h_lo = (h2 - h_hi.astype(F32)).astype(BF)
    both = jnp.dot(h_hi, wr_ref[...], preferred_element_type=F32)
    logits = (both[:, :LANES] + both[:, LANES:]
              + jnp.dot(h_lo, wr_ref[:, :LANES], preferred_element_type=F32) + br_ref[...])
    lane = lax.broadcasted_iota(jnp.int32, logits.shape, 1)
    vals, idxs = [], []
    l = logits
    for _ in range(TOP_K):
        mx = jnp.max(l, axis=-1, keepdims=True)
        idx = jnp.min(jnp.where(l == mx, lane, LANES), axis=-1, keepdims=True)
        vals.append(mx)
        idxs.append(idx)
        l = jnp.where(lane == idx, -jnp.inf, l)
    es = [jnp.exp(v - vals[0]) for v in vals]
    tot = es[0] + es[1] + es[2] + es[3]
    ti = jnp.zeros(logits.shape, jnp.int32)
    gw = jnp.zeros(logits.shape, F32)
    for k in range(TOP_K):
        ti = jnp.where(lane == k, idxs[k], ti)
        gw = jnp.where(lane == k, es[k] / tot, gw)
    topi_o[...] = ti
    gatew_o[...] = gw


def _merge(x2d, o_mla, o_diff, o_mem, lam, mw, tm):
    t = x2d.shape[0]
    row = lambda w: pl.BlockSpec((tm, w), lambda i: (i, 0))
    wnames = ["wg", "wmla", "wdiff", "wmem", "wout", "gattn", "gsub", "gffn", "wr", "br"]
    ws = [mw[k] for k in wnames]
    return pl.pallas_call(
        _merge_kernel,
        grid=(t // tm,),
        in_specs=[pl.BlockSpec(memory_space=pltpu.SMEM), row(D_MODEL), row(o_mla.shape[1]), row(o_diff.shape[1]),
                  row(o_mem.shape[1])] + [_const_spec(w.shape) for w in ws],
        out_specs=[row(D_MODEL), row(D_MODEL), row(LANES), row(LANES)],
        out_shape=[jax.ShapeDtypeStruct((t, D_MODEL), F32), jax.ShapeDtypeStruct((t, D_MODEL), F32),
                   jax.ShapeDtypeStruct((t, LANES), jnp.int32), jax.ShapeDtypeStruct((t, LANES), F32)],
        compiler_params=_cparams(("parallel",)),
        name="merge",
    )(lam, x2d, o_mla, o_diff, o_mem, *ws)


def _dispatch_kernel(pend_ref, dest_ref, h_ref, xs_hbm, zero_sc, sem, *, tm, tmg):
    i = pl.program_id(0)
    n_tiles = xs_hbm.shape[0] // tmg

    @pl.when(i == 0)
    def _():
        zero_sc[...] = jnp.zeros(zero_sc.shape, F32)

        def tail(e):
            end = pend_ref[e + 1]
            return pl.multiple_of(end - tmg, tmg), end > pend_ref[e]

        for e in range(N_EXPERTS):
            start, nonempty = tail(e)

            @pl.when(nonempty)
            def _():
                pltpu.make_async_copy(zero_sc, xs_hbm.at[pl.ds(start, tmg)], sem).start()
        for e in range(N_EXPERTS):
            start, nonempty = tail(e)

            @pl.when(nonempty)
            def _():
                pltpu.make_async_copy(zero_sc, xs_hbm.at[pl.ds(start, tmg)], sem).wait()

        def unused(tile):
            return pltpu.make_async_copy(zero_sc, xs_hbm.at[pl.ds(pl.multiple_of(tile * tmg, tmg), tmg)], sem)

        first_unused = pend_ref[N_EXPERTS] // tmg
        lax.fori_loop(first_unused, n_tiles, lambda tile, c: (unused(tile).start(), c)[1], 0)
        lax.fori_loop(first_unused, n_tiles, lambda tile, c: (unused(tile).wait(), c)[1], 0)

    def issue(rb, c):
        for u in range(SUBLANES):
            for k in range(TOP_K):
                dst = xs_hbm.at[pl.ds(dest_ref[0, 0, (rb * SUBLANES + u) * TOP_K + k], 1)]
                pltpu.make_async_copy(h_ref.at[rb, pl.ds(u, 1)], dst, sem).start(priority=k % 2)
        return c

    lax.fori_loop(0, tm // SUBLANES, issue, 0)
    for _ in range(TOP_K):
        pltpu.make_async_copy(xs_hbm.at[pl.ds(0, tm)], xs_hbm.at[pl.ds(0, tm)], sem).wait()


def _dispatch(h2, dest, pend0, n_rows, tm, tmg):
    t = h2.shape[0]
    dest3 = dest.reshape(t // tm, 1, tm * TOP_K)
    return pl.pallas_call(
        functools.partial(_dispatch_kernel, tm=tm, tmg=tmg),
        grid_spec=pltpu.PrefetchScalarGridSpec(
            num_scalar_prefetch=1,
            grid=(t // tm,),
            in_specs=[pl.BlockSpec((1, 1, tm * TOP_K), lambda i, p: (i, 0, 0), memory_space=pltpu.SMEM),
                      pl.BlockSpec((tm // SUBLANES, SUBLANES, D_MODEL), lambda i, p: (i, 0, 0))],
            out_specs=pl.BlockSpec(memory_space=pl.ANY),
            scratch_shapes=[pltpu.VMEM((tmg, D_MODEL), F32), pltpu.SemaphoreType.DMA(())],
        ),
        out_shape=jax.ShapeDtypeStruct((n_rows, D_MODEL), F32),
        compiler_params=_cparams(("arbitrary",)),
        name="moe_dispatch",
    )(pend0, dest3, h2.reshape(t // SUBLANES, SUBLANES, D_MODEL))


def _experts_kernel(te_ref, nu_ref, x_ref, wgu_ref, bgu_ref, wd_ref, bd_ref, o_ref):
    i = pl.program_id(0)

    @pl.when(i < nu_ref[0])
    def _():
        x = x_ref[...].astype(BF)
        gu = jnp.dot(x, wgu_ref[...], preferred_element_type=F32) + bgu_ref[...]
        gate = jnp.minimum(gu[:, :D_EXPERT], SWIGLU_LIMIT)
        up = jnp.clip(gu[:, D_EXPERT:], -SWIGLU_LIMIT, SWIGLU_LIMIT)
        act = (up + 1.0) * (gate * jax.nn.sigmoid(SWIGLU_ALPHA * gate))
        o_ref[...] = jnp.dot(act.astype(BF), wd_ref[...], preferred_element_type=F32) + bd_ref[...]

    @pl.when(i >= nu_ref[0])
    def _():
        o_ref[...] = jnp.zeros(o_ref.shape, F32)


def _experts(xs, tile_expert, n_used, wgu, bgu, wd, bd, tmg):
    n_rows = xs.shape[0]
    n_tiles = n_rows // tmg
    return pl.pallas_call(
        _experts_kernel,
        grid_spec=pltpu.PrefetchScalarGridSpec(
            num_scalar_prefetch=2,
            grid=(n_tiles,),
            in_specs=[pl.BlockSpec((tmg, D_MODEL), lambda i, te, nu: (jnp.minimum(i, nu[0] - 1), 0)),
                      pl.BlockSpec((None, D_MODEL, 2 * D_EXPERT), lambda i, te, nu: (te[i], 0, 0)),
                      pl.BlockSpec((None, 1, 2 * D_EXPERT), lambda i, te, nu: (te[i], 0, 0)),
                      pl.BlockSpec((None, D_EXPERT, D_MODEL), lambda i, te, nu: (te[i], 0, 0)),
                      pl.BlockSpec((None, 1, D_MODEL), lambda i, te, nu: (te[i], 0, 0))],
            out_specs=pl.BlockSpec((tmg, D_MODEL), lambda i, te, nu: (i, 0)),
        ),
        out_shape=jax.ShapeDtypeStruct((n_rows, D_MODEL), F32),
        compiler_params=_cparams(("arbitrary",)),
        name="moe_experts",
    )(tile_expert, n_used, xs, wgu, bgu, wd, bd)


def _combine_kernel(dest_ref, destn_ref, x2_ref, gw_ref, out_hbm, y_ref, buf, sem, *, tm):
    i = pl.program_id(0)
    n = pl.num_programs(0)

    def gather(idx_ref, slot):
        def issue(rb, c):
            for u in range(SUBLANES):
                for k in range(TOP_K):
                    src = out_hbm.at[pl.ds(idx_ref[0, 0, (rb * SUBLANES + u) * TOP_K + k], 1)]
                    pltpu.make_async_copy(src, buf.at[slot, k, rb, pl.ds(u, 1)],
                                          sem.at[slot]).start(priority=k % 2)
            return c

        lax.fori_loop(0, tm // SUBLANES, issue, 0)

    slot = i % 2

    @pl.when(i == 0)
    def _():
        gather(dest_ref, 0)

    @pl.when(i + 1 < n)
    def _():
        gather(destn_ref, 1 - slot)

    for k in range(TOP_K):
        pltpu.make_async_copy(out_hbm.at[pl.ds(0, tm)], out_hbm.at[pl.ds(0, tm)], sem.at[slot]).wait()
    gw = gw_ref[...]
    y = x2_ref[...]
    for k in range(TOP_K):
        y = y + gw[:, k:k + 1] * buf[slot, k].reshape(tm, D_MODEL)
    y_ref[...] = y


def _combine(x2, gate_w, dest, out_rows, tm):
    t = x2.shape[0]
    nt = t // tm
    dest3 = dest.reshape(nt, 1, tm * TOP_K)
    idx = lambda f: pl.BlockSpec((1, 1, tm * TOP_K), f, memory_space=pltpu.SMEM)
    return pl.pallas_call(
        functools.partial(_combine_kernel, tm=tm),
        grid=(nt,),
        in_specs=[idx(lambda i: (i, 0, 0)), idx(lambda i: (jnp.minimum(i + 1, nt - 1), 0, 0)),
                  pl.BlockSpec((tm, D_MODEL), lambda i: (i, 0)),
                  pl.BlockSpec((tm, LANES), lambda i: (i, 0)),
                  pl.BlockSpec(memory_space=pl.ANY)],
        out_specs=pl.BlockSpec((tm, D_MODEL), lambda i: (i, 0)),
        out_shape=jax.ShapeDtypeStruct((t, D_MODEL), F32),
        scratch_shapes=[pltpu.VMEM((2, TOP_K, tm // SUBLANES, SUBLANES, D_MODEL), F32),
                        pltpu.SemaphoreType.DMA((2,))],
        compiler_params=_cparams(("arbitrary",)),
        name="moe_combine",
    )(dest3, dest3, x2, gate_w, out_rows)


def _route(topi, tmg, n_tiles):
    e_ids = jnp.arange(N_EXPERTS, dtype=jnp.int32)
    hit = (topi[:, :, None] == e_ids[None, None, :])
    onehot = jnp.any(hit, axis=1).astype(jnp.int32)
    counts = jnp.sum(onehot, axis=0)
    pos = jnp.cumsum(onehot, axis=0) - onehot
    padded = (counts + tmg - 1) // tmg * tmg
    pend = jnp.cumsum(padded)
    pstart = pend - padded
    base = pstart[None, :] + pos
    dest = jnp.sum(jnp.where(hit, base[:, None, :], 0), axis=-1).astype(jnp.int32)
    n_used = (pend[-1] // tmg).astype(jnp.int32).reshape(1)
    tile_row0 = jnp.arange(n_tiles, dtype=jnp.int32) * tmg
    tile_expert = jnp.minimum(jnp.sum((pend[None, :] <= tile_row0[:, None]).astype(jnp.int32), axis=1),
                              N_EXPERTS - 1).astype(jnp.int32)
    pend0 = jnp.concatenate([jnp.zeros((1,), jnp.int32), pend.astype(jnp.int32)])
    return dest, pend0, n_used, tile_expert


def _moe(x2, h2, topi_pad, gate_w, ew, tm_tok, tmg):
    t = x2.shape[0]
    n_tiles = (t * TOP_K + N_EXPERTS * (tmg - 1) + tmg - 1) // tmg
    dest, pend0, n_used, tile_expert = _route(topi_pad[:, :TOP_K], tmg, n_tiles)
    xs = _dispatch(h2, dest.reshape(-1), pend0, n_tiles * tmg, tm_tok, tmg)
    out_rows = _experts(xs, tile_expert, n_used, ew["wgu"], ew["bgu"], ew["wd"], ew["bd"], tmg)
    return _combine(x2, gate_w, dest.reshape(-1), out_rows, tm_tok)


def _absorb_kernel(qm_ref, wukt_ref, o_ref):
    for hh in range(MLA_HEADS):
        qn = qm_ref[:, hh * LANES:hh * LANES + MLA_NOPE]
        o_ref[hh] = jnp.dot(qn, wukt_ref[hh], preferred_element_type=F32).astype(BF)


def _absorb(qm, wukt):
    ns = qm.shape[0]
    return pl.pallas_call(
        _absorb_kernel,
        in_specs=[_const_spec(qm.shape), _const_spec(wukt.shape)],
        out_specs=_const_spec((MLA_HEADS, ns, KV_RANK)),
        out_shape=jax.ShapeDtypeStruct((MLA_HEADS, ns, KV_RANK), BF),
        grid=(1,),
        compiler_params=_cparams(("arbitrary",)),
        name="sample_q_absorb",
    )(qm, wukt)


def _paged_kernel(pt_ref, qlat_ref, qr_ref, qbd_ref, bias_ref, lat_hbm, krt_hbm, kt_hbm, v_hbm, oa_ref, od_ref,
                  lat_buf, krt_buf, kt_buf, v_buf, sem, ma, la, acca, md, ld, accd, *, pb, nbuf):
    n = pl.program_id(0)
    j = pl.program_id(1)
    nj = pl.num_programs(1)
    step = n * nj + j
    total = pl.num_programs(0) * nj
    streams = ((lat_hbm, lat_buf), (krt_hbm, krt_buf), (kt_hbm, kt_buf), (v_hbm, v_buf))

    def start_fetch(s):
        sn = lax.div(s, nj)
        sj = lax.rem(s, nj)
        slot = lax.rem(s, nbuf)
        for p in range(pb):
            pg = pt_ref[sn, sj * pb + p]
            for hbm, buf in streams:
                pltpu.make_async_copy(hbm.at[pg], buf.at[slot, p], sem.at[slot]).start()

    def wait_fetch(slot):
        for hbm, buf in streams:
            pltpu.make_async_copy(hbm.at[pl.ds(0, pb)], buf.at[slot], sem.at[slot]).wait()

    @pl.when(step == 0)
    def _():
        for s0 in range(nbuf - 1):
            @pl.when(s0 < total)
            def _():
                start_fetch(jnp.int32(s0))

    @pl.when(step + (nbuf - 1) < total)
    def _():
        start_fetch(step + (nbuf - 1))

    slot = lax.rem(step, nbuf)
    wait_fetch(slot)

    class _SlotPages:
        def __init__(self, buf):
            self.buf = buf

        def __getitem__(self, p):
            return self.buf.at[slot, p]

    lat_refs, krt_refs, kt_refs, v_refs = (_SlotPages(b) for b in (lat_buf, krt_buf, kt_buf, v_buf))

    @pl.when(j == 0)
    def _():
        ma[...] = jnp.full(ma.shape, -jnp.inf, F32)
        la[...] = jnp.zeros(la.shape, F32)
        acca[...] = jnp.zeros(acca.shape, F32)
        md[...] = jnp.full(md.shape, -jnp.inf, F32)
        ld[...] = jnp.zeros(ld.shape, F32)
        accd[...] = jnp.zeros(accd.shape, F32)

    qlat = qlat_ref[...]
    qr = qr_ref[...]
    qbd = qbd_ref[...]
    rowh = lax.broadcasted_iota(jnp.int32, (2 * DIFF_HEADS, LANES), 0) // 2
    page = lambda a, p: a[:, p * PAGE_SIZE:(p + 1) * PAGE_SIZE]

    s_a = jnp.concatenate(
        [lax.dot_general(qlat, lat_refs[p][...].astype(BF), (((1,), (1,)), ((), ())), preferred_element_type=F32)
         + jnp.dot(qr, krt_refs[p][...].astype(BF), preferred_element_type=F32) for p in range(pb)], axis=1)
    s_d = jnp.concatenate([jnp.dot(qbd, kt_refs[p][...].astype(BF), preferred_element_type=F32) for p in range(pb)],
                          axis=1) + bias_ref[...]

    def softmax_update(s, m_ref, l_ref):
        m_new = jnp.maximum(m_ref[...], jnp.max(s, axis=-1, keepdims=True))
        alpha = jnp.exp(m_ref[...] - m_new)
        pe = jnp.exp(s - m_new)
        l_ref[...] = alpha * l_ref[...] + jnp.sum(pe, axis=-1, keepdims=True)
        m_ref[...] = m_new
        return alpha, pe.astype(BF)

    alpha_a, pe_a = softmax_update(s_a, ma, la)
    alpha_d, pe_d = softmax_update(s_d, md, ld)

    pv = jnp.dot(page(pe_a, 0), lat_refs[0][...].astype(BF), preferred_element_type=F32)
    for p in range(1, pb):
        pv = pv + jnp.dot(page(pe_a, p), lat_refs[p][...].astype(BF), preferred_element_type=F32)
    acca[...] = alpha_a * acca[...] + pv

    pv = jnp.zeros((2 * DIFF_HEADS, DIFF_VD), F32)
    for hh in range(DIFF_HEADS):
        pvh = None
        for p in range(pb):
            vh = v_refs[p][pl.ds(hh, PAGE_SIZE, stride=DIFF_HEADS), :].astype(BF)
            part = jnp.dot(page(pe_d, p), vh, preferred_element_type=F32)
            pvh = part if pvh is None else pvh + part
        pv = jnp.where(rowh == hh, pvh, pv)
    accd[...] = alpha_d * accd[...] + pv

    @pl.when(j == pl.num_programs(1) - 1)
    def _():
        oa_ref[:, 0:KV_RANK] = acca[...]
        oa_ref[:, KV_RANK:KV_RANK + LANES] = jnp.broadcast_to(ma[...], (MLA_HEADS, LANES))
        oa_ref[:, KV_RANK + LANES:] = jnp.broadcast_to(la[...], (MLA_HEADS, LANES))
        od_ref[:, 0:DIFF_VD] = accd[...]
        od_ref[:, DIFF_VD:2 * DIFF_VD] = jnp.broadcast_to(md[...], (2 * DIFF_HEADS, LANES))
        od_ref[:, 2 * DIFF_VD:] = jnp.broadcast_to(ld[...], (2 * DIFF_HEADS, LANES))


def _paged_attention(page_table, qlat, qr, qbd, bias, lat_pages, krt_pages, kt_pages, v_pages, pb):
    ns, n_pages = page_table.shape
    assert n_pages % pb == 0

    per_n = lambda shape: pl.BlockSpec((None,) + shape, lambda n, j, pt: (n, 0, 0))
    in_specs = [per_n((MLA_HEADS, KV_RANK)), per_n((MLA_HEADS, MLA_ROPE)), per_n((2 * DIFF_HEADS, 512)),
                pl.BlockSpec((2 * DIFF_HEADS, pb * PAGE_SIZE), lambda n, j, pt: (0, j))]
    in_specs += [pl.BlockSpec(memory_space=pl.ANY)] * 4
    ins = [qlat, qr, qbd, bias, lat_pages, krt_pages, kt_pages, v_pages]
    nbuf = PAGE_RING_SLOTS
    ring = lambda shape: pltpu.VMEM((nbuf, pb) + shape, F32)
    wa = KV_RANK + 2 * LANES
    wd = 3 * DIFF_VD
    return pl.pallas_call(
        functools.partial(_paged_kernel, pb=pb, nbuf=nbuf),
        grid_spec=pltpu.PrefetchScalarGridSpec(
            num_scalar_prefetch=1,
            grid=(ns, n_pages // pb),
            in_specs=in_specs,
            out_specs=[per_n((MLA_HEADS, wa)), per_n((2 * DIFF_HEADS, wd))],
            scratch_shapes=[ring((PAGE_SIZE, KV_RANK)), ring((MLA_ROPE, PAGE_SIZE)), ring((512, PAGE_SIZE)),
                            ring((512, DIFF_VD)), pltpu.SemaphoreType.DMA((nbuf,)),
                            pltpu.VMEM((MLA_HEADS, 1), F32), pltpu.VMEM((MLA_HEADS, 1), F32),
                            pltpu.VMEM((MLA_HEADS, KV_RANK), F32),
                            pltpu.VMEM((2 * DIFF_HEADS, 1), F32), pltpu.VMEM((2 * DIFF_HEADS, 1), F32),
                            pltpu.VMEM((2 * DIFF_HEADS, DIFF_VD), F32)],
        ),
        out_shape=[jax.ShapeDtypeStruct((ns, MLA_HEADS, wa), F32),
                   jax.ShapeDtypeStruct((ns, 2 * DIFF_HEADS, wd), F32)],
        compiler_params=_cparams(("arbitrary", "arbitrary")),
        name="sample_paged_attn",
    )(page_table, *ins)


def _finish_kernel(b0_ref, pa_ref, pd_ref, qlat_ref, qm_ref, qd_ref, ckv_ref, kr_ref, kd_ref, vd_ref, wuv_ref,
                   omla_o, odiff_o):
    ckv = ckv_ref[...]
    kr = kr_ref[...]
    krs = pltpu.roll(kr, MLA_NOPE, 1)
    lane = lax.broadcasted_iota(jnp.int32, (1, LANES), 1)
    rope_lanes = (lane >= MLA_NOPE) & (lane < MLA_QK)
    for hh in range(MLA_HEADS):
        acc = pa_ref[hh, :, 0:KV_RANK]
        m = pa_ref[hh, :, KV_RANK:KV_RANK + 1]
        l = pa_ref[hh, :, KV_RANK + LANES:KV_RANK + LANES + 1]
        qg = qm_ref[:, hh * LANES:(hh + 1) * LANES].astype(F32)
        s = (jnp.sum(qlat_ref[hh].astype(F32) * ckv, axis=-1, keepdims=True)
             + jnp.sum(jnp.where(rope_lanes, qg * krs, 0.0), axis=-1, keepdims=True))
        m_new = jnp.maximum(m, s)
        alpha = jnp.exp(m - m_new)
        pn = jnp.exp(s - m_new)
        l = alpha * l + pn
        acc = alpha * acc + pn * ckv
        o_lat = (acc / l).astype(BF)
        omla_o[:, hh * MLA_VD:(hh + 1) * MLA_VD] = jnp.dot(
            o_lat, wuv_ref[:, hh * MLA_VD:(hh + 1) * MLA_VD], preferred_element_type=F32).astype(BF)
    for g in range(2 * DIFF_HEADS):
        hh = g // 2
        acc = pd_ref[g, :, 0:DIFF_VD]
        m = pd_ref[g, :, DIFF_VD:DIFF_VD + 1]
        l = pd_ref[g, :, 2 * DIFF_VD:2 * DIFF_VD + 1]
        qg = qd_ref[:, g * LANES:(g + 1) * LANES].astype(F32)
        s = jnp.sum(qg * kd_ref[:, hh * LANES:(hh + 1) * LANES], axis=-1, keepdims=True) + b0_ref[hh]
        m_new = jnp.maximum(m, s)
        alpha = jnp.exp(m - m_new)
        pn = jnp.exp(s - m_new)
        l = alpha * l + pn
        acc = alpha * acc + pn * vd_ref[:, hh * LANES:(hh + 1) * LANES]
        odiff_o[:, g * LANES:(g + 1) * LANES] = acc / l


def _finish(b0, pa, pd, qlat, qm, qd, ckv, kr128, kd, vd, wuv):
    ns = qm.shape[0]
    ins = [pa, pd, qlat, qm, qd, ckv, kr128, kd, vd, wuv]
    return pl.pallas_call(
        _finish_kernel,
        grid=(1,),
        in_specs=[pl.BlockSpec(memory_space=pltpu.SMEM)] + [_const_spec(a.shape) for a in ins],
        out_specs=[_const_spec((ns, MLA_HEADS * MLA_VD)), _const_spec((ns, 2 * DIFF_HEADS * LANES))],
        out_shape=[jax.ShapeDtypeStruct((ns, MLA_HEADS * MLA_VD), BF),
                   jax.ShapeDtypeStruct((ns, 2 * DIFF_HEADS * LANES), F32)],
        compiler_params=_cparams(("arbitrary",)),
        name="sample_attn_finish",
    )(b0, *ins)


def _mem_decode_kernel(q_ref, k_ref, v_ref, o_ref, *, m_len):
    q = q_ref[...]
    rows = lax.broadcasted_iota(jnp.int32, (8, LANES), 0)
    out = jnp.zeros((8, LANES), F32)
    for hh in range(MEM_HEADS):
        kh = k_ref[pl.ds(hh, m_len, stride=MEM_HEADS), :].astype(BF)
        vh = v_ref[pl.ds(hh, m_len, stride=MEM_HEADS), :].astype(BF)
        s = lax.dot_general(q, kh, (((1,), (1,)), ((), ())), preferred_element_type=F32)
        m = jnp.max(s, axis=-1, keepdims=True)
        pe = jnp.exp(s - m)
        l = jnp.sum(pe, axis=-1, keepdims=True)
        o = jnp.dot(pe.astype(BF), vh, preferred_element_type=F32) / l
        out = jnp.where(rows == hh, o, out)
    o_ref[...] = out.astype(o_ref.dtype)


def _mem_decode(q8, mem_k, mem_v, m_len):
    ns = q8.shape[0]
    blk = lambda r: pl.BlockSpec((None, r, LANES), lambda n: (n, 0, 0))
    return pl.pallas_call(
        functools.partial(_mem_decode_kernel, m_len=m_len),
        grid=(ns,),
        in_specs=[blk(8), blk(m_len * MEM_HEADS), blk(m_len * MEM_HEADS)],
        out_specs=blk(8),
        out_shape=jax.ShapeDtypeStruct((ns, 8, LANES), BF),
        compiler_params=_cparams(("parallel",)),
        name="sample_mem_attn",
    )(q8, mem_k, mem_v)


def _sample_path(x_sample, caches, page_table, t5_bias, pw, lam, w_uk, w_uv):
    cache_lat, cache_kr, cache_dk, cache_dv, cache_mk, cache_mv = caches
    ns, t1, d = x_sample.shape
    assert t1 == 1, "one new token per sample"
    n_pages = page_table.shape[1]
    pos = jnp.full((ns,), n_pages * PAGE_SIZE, jnp.int32)
    (qm, ckv, kr, qd, kd, kdb, vd, vdb, qmem) = _project(
        x_sample.reshape(ns, d), _rope_lane_tables(pos), pw["wp"], pw["gains"], ns, False)
    n_pool = cache_lat.shape[1]
    lat_pages = cache_lat[0]
    krt_pages = jnp.transpose(cache_kr[0], (0, 2, 1))
    kt_pages = jnp.transpose(cache_dk[0], (0, 2, 3, 4, 1)).reshape(n_pool, 512, PAGE_SIZE)
    v_pages = cache_dv[0].reshape(n_pool, PAGE_SIZE * DIFF_HEADS, DIFF_VD)
    m_len = cache_mk.shape[2]
    mem_k = cache_mk[0].reshape(ns, m_len * MEM_HEADS, MEM_HD)
    mem_v = cache_mv[0].reshape(ns, m_len * MEM_HEADS, MEM_HD)

    wukt = jnp.transpose(w_uk, (1, 2, 0)).astype(BF)
    qlat_h = _absorb(qm, wukt)
    qlat = jnp.transpose(qlat_h, (1, 0, 2))
    qr = qm.reshape(ns, MLA_HEADS, LANES)[:, :, MLA_NOPE:MLA_QK]
    qd4 = qd.reshape(ns, DIFF_HEADS, 2, LANES)
    z = jnp.zeros_like(qd4)
    qbd = jnp.stack([jnp.where(jnp.arange(DIFF_HEADS)[None, :, None, None] == hh, qd4, z)
                     for hh in range(DIFF_HEADS)], axis=3)
    qbd = qbd.reshape(ns, 2 * DIFF_HEADS, DIFF_HEADS * LANES)
    q_pos = n_pages * PAGE_SIZE
    bias = _t5_bucket_bias(t5_bias, q_pos - jnp.arange(n_pages * PAGE_SIZE))
    bias = jnp.repeat(bias, 2, axis=0)
    b0 = _t5_bucket_bias(t5_bias, jnp.zeros((1,), jnp.int32))[:, 0]
    pa, pd = _paged_attention(page_table, qlat, qr, qbd, bias, lat_pages, krt_pages, kt_pages, v_pages, pw["pb"])
    kr128 = jnp.pad(kr, ((0, 0), (0, LANES - MLA_ROPE)))
    o_mla, o_diff = _finish(b0, jnp.transpose(pa, (1, 0, 2)), jnp.transpose(pd, (1, 0, 2)), qlat_h, qm, qd, ckv,
                            kr128, kd, vd.reshape(ns, DIFF_HEADS * DIFF_VD), pw["wuv_c"])
    q8 = jnp.pad(qmem.reshape(ns, MEM_HEADS, MEM_HD), ((0, 0), (0, 8 - MEM_HEADS), (0, 0)))
    o_mem = _mem_decode(q8, mem_k, mem_v, m_len)[:, :MEM_HEADS].reshape(ns, MEM_HEADS * MEM_HD)
    x2, h2, topi, gw = _merge(x_sample.reshape(ns, d), o_mla, o_diff, o_mem, lam, pw["merge"], ns)
    y = _moe(x2, h2, topi, gw, pw["experts"], ns, pw["tmg_sample"])
    return (y.reshape(ns, 1, d), ckv.reshape(1, ns, 1, KV_RANK), kr.reshape(1, ns, 1, MLA_ROPE),
            kd.reshape(1, ns, 1, DIFF_HEADS, 2, DIFF_HD), vd.reshape(1, ns, 1, DIFF_HEADS, DIFF_VD))


def _prompt_path(x_prompt, mem_prompt, t5_bias, pw, lam):
    n, s, d = x_prompt.shape
    t = n * s
    pos_tab = _rope_lane_tables(jnp.arange(s))
    (qm, ckv, kr, qd, kd, kdb, vd, vdb, qmem, kmla, vmla) = _project(
        x_prompt.reshape(t, d), pos_tab, pw["wp"], pw["gains"], pw["tm_proj"], True, pw["wuk"], pw["wuv"])
    tq = pw["tq"]
    r3 = lambda a: a.reshape(n, s, a.shape[-1])
    o_mla = _flash(r3(qm), r3(kmla), r3(vmla), None, causal=True, kv_shared=False, hps=2, tq=tq, tk=tq,
                   out_dtype=BF, name="attn_mla", pack64=True)
    o_diff = _flash(r3(qd), r3(kdb), r3(vdb), _diff_bias_tiles(t5_bias, tq), causal=True, kv_shared=True, hps=2,
                    tq=tq, tk=tq, out_dtype=F32, name="attn_diff")
    m_len = mem_prompt.shape[1]
    mk, mv, mkb, mvb = _memory_kv(mem_prompt.reshape(n * m_len, d), pw["w_mem_kv"], pw["g_mem"], pw["g_k_mem"],
                                  min(512, n * m_len))
    rm = lambda a: a.reshape(n, m_len, a.shape[-1])
    o_mem = _flash(r3(qmem), rm(mkb), rm(mvb), None, causal=False, kv_shared=False, hps=2, tq=tq, tk=m_len,
                   out_dtype=BF, name="attn_mem")
    x2, h2, topi, gw = _merge(x_prompt.reshape(t, d), o_mla.reshape(t, -1), o_diff.reshape(t, -1),
                              o_mem.reshape(t, -1), lam, pw["merge"], pw["tm_merge"])
    y = _moe(x2, h2, topi, gw, pw["experts"], pw["tm_tok"], pw["tmg"])
    outs = (y.reshape(n, s, d), ckv.reshape(1, n, s, KV_RANK), jnp.transpose(kr, (0, 2, 1))[None],
            jnp.transpose(kd.reshape(1, n, DIFF_HEADS, 2, DIFF_HD, s), (0, 1, 5, 2, 3, 4)),
            vd.reshape(1, n, s, DIFF_HEADS, DIFF_VD),
            mk.reshape(1, n, m_len, MEM_HEADS, MEM_HD), mv.reshape(1, n, m_len, MEM_HEADS, MEM_HD))
    return outs


def _prepare(g_attn, w_in, g_q_mla, g_ckv, g_krope, w_uk, w_uv, g_q_diff, g_k_diff, g_subln, g_mem, w_mem_kv,
             g_q_mem, g_k_mem, w_br_mla, w_br_diff, w_br_mem, w_out, g_ffn, w_router, b_router, w_gate_up,
             b_gate_up, w_down, b_down):
    wp, wg = _pack_proj_weights(w_in, g_q_mla, g_krope)
    gains = [g_attn.reshape(1, -1),
             _pad_lanes(g_q_mla * (MLA_QK ** -0.5)),
             g_ckv.reshape(1, -1),
             _pad_lanes(g_krope),
             jnp.tile(g_q_diff * (DIFF_HD ** -0.5), 2).reshape(1, -1),
             jnp.tile(g_k_diff, 2).reshape(1, -1),
             (g_q_mem * (MEM_HD ** -0.5)).reshape(1, -1)]
    wuk = jnp.pad(w_uk, ((0, 0), (0, 0), (0, LANES - MLA_NOPE))).reshape(KV_RANK, MLA_HEADS * LANES).astype(BF)
    wuv = jnp.pad(w_uv, ((0, 0), (0, 0), (0, LANES - MLA_VD))).reshape(KV_RANK, MLA_HEADS * LANES).astype(BF)
    wmla = w_br_mla.astype(BF)
    wuv_c = w_uv.reshape(KV_RANK, MLA_HEADS * MLA_VD).astype(BF)
    wr = jnp.pad(w_router, ((0, 0), (0, LANES - N_EXPERTS)))
    wr_hi = wr.astype(BF)
    wr = jnp.concatenate([wr_hi, (wr - wr_hi.astype(F32)).astype(BF)], axis=1)
    br = jnp.concatenate([b_router, jnp.full((LANES - N_EXPERTS,), NEG_BIG, F32)]).reshape(1, LANES)
    merge = dict(wg=wg, wmla=wmla, wdiff=w_br_diff.astype(BF), wmem=w_br_mem.astype(BF), wout=w_out.astype(BF),
                 gattn=g_attn.reshape(1, -1), gsub=g_subln.reshape(1, -1), gffn=g_ffn.reshape(1, -1), wr=wr, br=br)
    experts = dict(wgu=w_gate_up.astype(BF), bgu=b_gate_up.reshape(N_EXPERTS, 1, -1), wd=w_down.astype(BF),
                   bd=b_down.reshape(N_EXPERTS, 1, -1))
    return dict(wp=wp, gains=gains, wuk=wuk, wuv=wuv, wuv_c=wuv_c, merge=merge, experts=experts, w_mem_kv=w_mem_kv,
                g_mem=g_mem, g_k_mem=g_k_mem)


def kernel(x_prompt, x_sample, mem_prompt, cache_mla_latent, cache_mla_krope, cache_diff_k, cache_diff_v,
           cache_mem_k, cache_mem_v, page_table, t5_bias, g_attn, w_in, g_q_mla, g_ckv, g_krope, w_uk, w_uv,
           g_q_diff, g_k_diff, lambda_q1, lambda_k1, lambda_q2, lambda_k2, g_subln, g_mem, w_mem_kv, g_q_mem,
           g_k_mem, w_br_mla, w_br_diff, w_br_mem, w_out, g_ffn, w_router, b_router, w_gate_up, b_gate_up,
           w_down, b_down):
    assert g_attn.shape[0] == 1, "single-layer trunk"
    l = 0
    pw = _prepare(g_attn[l], w_in[l], g_q_mla[l], g_ckv[l], g_krope[l], w_uk[l], w_uv[l], g_q_diff[l], g_k_diff[l],
                  g_subln[l], g_mem[l], w_mem_kv[l], g_q_mem[l], g_k_mem[l], w_br_mla[l], w_br_diff[l],
                  w_br_mem[l], w_out[l], g_ffn[l], w_router[l], b_router[l], w_gate_up[l], b_gate_up[l],
                  w_down[l], b_down[l])
    lam = (jnp.exp(jnp.sum(lambda_q1[l] * lambda_k1[l]).astype(F32))
           - jnp.exp(jnp.sum(lambda_q2[l] * lambda_k2[l]).astype(F32)) + LAM_INIT).reshape(1)
    s = x_prompt.shape[1]
    pw.update(tm_proj=min(TILE_PROJ, s), tq=min(TILE_ATTN, s), tm_merge=min(TILE_MERGE, s),
              tm_tok=min(TILE_TOKEN, s), tmg=TILE_EXPERT, tmg_sample=TILE_EXPERT_SAMPLE, pb=PAGES_PER_STEP)
    p = _prompt_path(x_prompt, mem_prompt, t5_bias, pw, lam)
    caches = (cache_mla_latent, cache_mla_krope, cache_diff_k, cache_diff_v, cache_mem_k, cache_mem_v)
    sm = _sample_path(x_sample, caches, page_table, t5_bias, pw, lam, w_uk[l], w_uv[l])
    return (p[0], sm[0]) + p[1:] + sm[1:]
```
